```python
import math
import jax, jax.numpy as jnp
from jax import lax
import numpy as np

D_MODEL = 1024
BATCH = 4
SEQ = 8192
DEPTH = 4

CTX_LEN = 256
GRID_W = 64
N_MIXERS = 4
ALPHA = (2 * DEPTH) ** 0.25
BETA = (8 * DEPTH) ** -0.25
LN_EPS = 1e-5
ATT_HEADS = 8
ATT_HEAD_DIM = D_MODEL // (2 * ATT_HEADS)
ROPE_THETA = 10000.0
Q_BLOCK = 128
POOL_WINDOWS = (2, 4, 8, 16)
POOL_GROUP = D_MODEL // len(POOL_WINDOWS)
FFT_GROUPS = 4
FFT_GROUP = D_MODEL // FFT_GROUPS
HY_BANDS = 16
HY_EMB = 2 * HY_BANDS + 1
HY_FILTER_HIDDEN = 64
HY_TARGET = 1e-2
HY_FAST = 0.3
HY_SLOW = 1.5
N_EXPERTS = 16
N_GROUPS = 4
EXPERTS_PER_GROUP = N_EXPERTS // N_GROUPS
TOP_K = 2
D_EXPERT = 1024
MOE_BLOCK = 256

kernel_name = "hybrid_diffusion_interleaved_moe_trunk"


def n_layers_of(kind):
    return len(range(kind, DEPTH, N_MIXERS))


def ctx_needed(i):
    return any(j % N_MIXERS == 0 for j in range(i, DEPTH))


def layer_norm(x, g, b):
    xf = x.astype(jnp.float32)
    mu = xf.mean(-1, keepdims=True)
    var = jnp.square(xf - mu).mean(-1, keepdims=True)
    return ((xf - mu) * lax.rsqrt(var + LN_EPS) * g.astype(jnp.float32) + b.astype(jnp.float32)).astype(x.dtype)


def deepnorm_update(x, y, gate, g, b):
    return layer_norm(ALPHA * x + (1.0 + gate) * y, g, b)


def axial_rope_tables(n_tokens):
    rows = n_tokens // GRID_W
    row = jnp.repeat(jnp.arange(rows), GRID_W).astype(jnp.float32)
    col = jnp.tile(jnp.arange(GRID_W), rows).astype(jnp.float32)
    half = ATT_HEAD_DIM // 2
    inv = ROPE_THETA ** (-jnp.arange(0, half, 2, dtype=jnp.float32) / half)
    ar = row[:, None] * inv
    ac = col[:, None] * inv
    ang = jnp.concatenate([ar, ar, ac, ac], axis=-1)
    return jnp.cos(ang), jnp.sin(ang)


def apply_axial_rope(x, cos, sin):
    a, b, c, d = jnp.split(x, 4, axis=-1)
    rot = jnp.concatenate([-b, a, -d, c], axis=-1)
    cos = cos[None, :, None, None, :].astype(x.dtype)
    sin = sin[None, :, None, None, :].astype(x.dtype)
    return x * cos + rot * sin


def attn_qkv(h, wqkv):
    B, L, _ = h.shape
    q, k, v = jnp.split(h @ wqkv, 3, axis=-1)
    q = q.reshape(B, L, ATT_HEADS, 2, ATT_HEAD_DIM)
    k = k.reshape(B, L, ATT_HEADS, 2, ATT_HEAD_DIM)
    v = v.reshape(B, L, ATT_HEADS, 2 * ATT_HEAD_DIM)
    return q, k, v


def diff_attention(q, k, v, lam, subln, wo, lam_init):
    B, Lq = q.shape[0], q.shape[1]
    lf = lam.astype(jnp.float32)
    lam_full = jnp.exp(jnp.sum(lf[0] * lf[1])) - jnp.exp(jnp.sum(lf[2] * lf[3])) + lam_init
    scale = ATT_HEAD_DIM ** -0.5

    def block(qb):
        s = jnp.einsum('bqhnd,bkhnd->nbhqk', qb, k).astype(jnp.float32) * scale
        p = jax.nn.softmax(s, axis=-1)
        a = p[0] - lam_full * p[1]
        return jnp.einsum('bhqk,bkhe->bqhe', a.astype(v.dtype), v)

    nb = Lq // Q_BLOCK
    qb = q.reshape(B, nb, Q_BLOCK, ATT_HEADS, 2, ATT_HEAD_DIM).swapaxes(0, 1)
    o = lax.map(block, qb)
    o = o.swapaxes(0, 1).reshape(B, Lq, ATT_HEADS, 2 * ATT_HEAD_DIM).astype(jnp.float32)
    o = o * lax.rsqrt(jnp.mean(jnp.square(o), axis=-1, keepdims=True) + LN_EPS)
    o = o * subln.astype(jnp.float32) * (1.0 - lam_init)
    return o.reshape(B, Lq, D_MODEL).astype(q.dtype) @ wo


def pool_mixer(h, w, scale):
    B, L, D = h.shape
    hf = h.astype(jnp.float32)
    csum = jnp.concatenate([jnp.zeros((B, 1, D), jnp.float32), jnp.cumsum(hf, axis=1)], axis=1)
    t = jnp.arange(L)
    groups = []
    for g, win in enumerate(POOL_WINDOWS):
        lo = jnp.clip(t - win // 2, 0, L)
        hi = jnp.clip(t - win // 2 + win, 0, L)
        sl = slice(g * POOL_GROUP, (g + 1) * POOL_GROUP)
        cs = csum[:, :, sl]
        mean = (cs[:, hi] - cs[:, lo]) / (hi - lo).astype(jnp.float32)[:, None]
        groups.append(mean - hf[:, :, sl])
    pooled = jnp.stack(groups, axis=2).astype(h.dtype)
    y = jnp.einsum('blgd,gde->blge', pooled, w).reshape(B, L, D)
    return y * scale


def fourier_mixer(h, w, b):
    B, L, D = h.shape
    hg = h.astype(jnp.float32).reshape(B, L, FFT_GROUPS, FFT_GROUP)
    f = jnp.fft.fft2(hg, axes=(1, 3), norm="ortho").real
    return f.reshape(B, L, D).astype(h.dtype) @ w + b


def hyena_filters(L, f1, fb1, f2, fb2, f3, freq):
    f32 = jnp.float32
    pos = jnp.arange(L, dtype=f32)
    t = pos / L
    bands = jnp.linspace(1e-4, HY_BANDS - 1, HY_BANDS, dtype=f32)
    ang = (2.0 * math.pi / L) * pos[:, None] * bands[None, :]
    z = jnp.concatenate([t[:, None], jnp.cos(ang), jnp.sin(ang)], axis=-1)
    fr = freq.astype(f32)
    a = jnp.sin(fr * (z @ f1.astype(f32) + fb1.astype(f32)))
    a = jnp.sin(fr * (a @ f2.astype(f32) + fb2.astype(f32)))
    k = a @ f3.astype(f32)
    deltas = jnp.abs(jnp.linspace(math.log(HY_TARGET) / HY_SLOW, math.log(HY_TARGET) / HY_FAST, D_MODEL, dtype=f32))
    window = jnp.exp(-t[:, None] * deltas[None, :])
    return k[:, :D_MODEL] * window, k[:, D_MODEL:] * window


def hyena_mixer(h, win, bin_, conv_w, conv_b, f1, fb1, f2, fb2, f3, freq, skip, wo, bo):
    B, L, D = h.shape
    z = h @ win + bin_
    zp = jnp.pad(z, ((0, 0), (1, 1), (0, 0)))
    z = zp[:, :-2] * conv_w[0] + zp[:, 1:-1] * conv_w[1] + zp[:, 2:] * conv_w[2] + conv_b
    x0, x1, v = jnp.split(z, 3, axis=-1)
    v = (v * x1).astype(jnp.float32)
    kf, kb = hyena_filters(L, f1, fb1, f2, fb2, f3, freq)
    k_full = jnp.concatenate([kf, jnp.zeros((1, D), jnp.float32), kb[:0:-1]], axis=0)
    n = 2 * L
    y = jnp.fft.irfft(jnp.fft.rfft(v, n=n, axis=1) * jnp.fft.rfft(k_full, n=n, axis=0)[None], n=n, axis=1)[:, :L]
    y = (y + v * skip.astype(jnp.float32)) * x0.astype(jnp.float32)
    return y.astype(h.dtype) @ wo + bo


def route(h, router_w, router_b):
    T = h.shape[0]
    logits = (h @ router_w).astype(jnp.float32) + router_b.astype(jnp.float32)
    probs = jax.nn.softmax(logits, axis=-1).reshape(T, N_GROUPS, EXPERTS_PER_GROUP)
    gscore = lax.top_k(probs, TOP_K)[0].sum(-1)
    g = jnp.argmax(gscore, axis=-1)
    in_g = probs[jnp.arange(T), g]
    vals, idx = lax.top_k(in_g, TOP_K)
    return g[:, None] * EXPERTS_PER_GROUP + idx, vals / vals.sum(-1, keepdims=True)


def moe_ffn(h, router_w, router_b, wg, wu, wd):
    T, D = h.shape
    eidx, ew = route(h, router_w, router_b)
    A = T * TOP_K
    flat_e = eidx.reshape(A)
    flat_w = ew.reshape(A)
    flat_t = jnp.repeat(jnp.arange(T, dtype=jnp.int32), TOP_K)
    order = jnp.argsort(flat_e)
    se, st, sw = flat_e[order], flat_t[order], flat_w[order]
    counts = jnp.bincount(flat_e, length=N_EXPERTS)
    starts = jnp.cumsum(counts) - counts
    padded = (counts + MOE_BLOCK - 1) // MOE_BLOCK * MOE_BLOCK
    pend = jnp.cumsum(padded)
    pstart = pend - padded
    dest = pstart[se] + jnp.arange(A) - starts[se]
    n_blocks = -(-A // MOE_BLOCK) + N_EXPERTS
    P = n_blocks * MOE_BLOCK
    buf_t = jnp.zeros((P,), jnp.int32).at[dest].set(st)
    buf_w = jnp.zeros((P,), jnp.float32).at[dest].set(sw)
    blk_e = jnp.minimum(jnp.searchsorted(pend, jnp.arange(n_blocks) * MOE_BLOCK, side='right'), N_EXPERTS - 1)
    xs = h[buf_t].reshape(n_blocks, MOE_BLOCK, D)

    def expert_block(args):
        xb, e = args
        return (jax.nn.silu(xb @ wg[e]) * (xb @ wu[e])) @ wd[e]

    ys = lax.map(expert_block, (xs, blk_e)).reshape(P, D)
    out = jnp.zeros((T, D), jnp.float32).at[buf_t].add(ys.astype(jnp.float32) * buf_w[:, None])
    return out.astype(h.dtype)


def setup_inputs(seed: int = 0) -> dict:
    key = jax.random.key(seed)
    ks = list(jax.random.split(key, 40))
    f32 = jnp.float32

    def nrm(i, shape, s):
        return jax.random.normal(ks[i], shape, f32) * s

    nA, nB, nC, nD = n_layers_of(0), n_layers_of(1), n_layers_of(2), n_layers_of(3)
    D = D_MODEL
    return {
        "x": nrm(0, (BATCH, SEQ, D), 1.0),
        "c": nrm(1, (BATCH, D), 1.0),
        "ctx": nrm(2, (BATCH, CTX_LEN, D), 1.0),
        "c_ctx": nrm(3, (D,), 1.0),
        "mod_w": nrm(4, (DEPTH, D, 6 * D), 0.3 * D ** -0.5),
        "mod_b": nrm(5, (DEPTH, 6 * D), 0.02),
        "ln_g": 1.0 + nrm(6, (DEPTH, 2, D), 0.02),
        "ln_b": nrm(7, (DEPTH, 2, D), 0.02),
        "attn_wqkv": nrm(8, (nA, D, 3 * D), D ** -0.5),
        "attn_wo": nrm(9, (nA, D, D), BETA * D ** -0.5),
        "attn_lam": nrm(10, (nA, 4, ATT_HEAD_DIM), 0.1),
        "attn_subln": 1.0 + nrm(11, (nA, 2 * ATT_HEAD_DIM), 0.02),
        "pool_w": nrm(12, (nB, len(POOL_WINDOWS), POOL_GROUP, POOL_GROUP), BETA * POOL_GROUP ** -0.5),
        "pool_scale": 1.0 + nrm(13, (nB, D), 0.02),
        "fnet_w": nrm(14, (nC, D, D), BETA * D ** -0.5),
        "fnet_b": nrm(15, (nC, D), 0.02),
        "hy_win": nrm(16, (nD, D, 3 * D), D ** -0.5),
        "hy_bin": nrm(17, (nD, 3 * D), 0.02),
        "hy_conv_w": nrm(18, (nD, 3, 3 * D), 3 ** -0.5),
        "hy_conv_b": nrm(19, (nD, 3 * D), 0.02),
        "hy_f1": nrm(20, (nD, HY_EMB, HY_FILTER_HIDDEN), HY_EMB ** -0.5),
        "hy_fb1": nrm(21, (nD, HY_FILTER_HIDDEN), 0.02),
        "hy_f2": nrm(22, (nD, HY_FILTER_HIDDEN, HY_FILTER_HIDDEN), HY_FILTER_HIDDEN ** -0.5),
        "hy_fb2": nrm(23, (nD, HY_FILTER_HIDDEN), 0.02),
        "hy_f3": nrm(24, (nD, HY_FILTER_HIDDEN, 2 * D), 0.02),
        "hy_freq": 1.0 + nrm(25, (nD, HY_FILTER_HIDDEN), 0.02),
        "hy_skip": nrm(26, (nD, D), 0.1),
        "hy_wo": nrm(27, (nD, D, D), BETA * D ** -0.5),
        "hy_bo": nrm(28, (nD, D), 0.02),
        "router_w": nrm(29, (D, N_EXPERTS), D ** -0.5),
        "router_b": nrm(30, (N_EXPERTS,), 0.01),
        "moe_wg": nrm(31, (DEPTH, N_EXPERTS, D, D_EXPERT), D ** -0.5),
        "moe_wu": nrm(32, (DEPTH, N_EXPERTS, D, D_EXPERT), D ** -0.5),
        "moe_wd": nrm(33, (DEPTH, N_EXPERTS, D_EXPERT, D), BETA * D_EXPERT ** -0.5),
    }


def reference(x, c, ctx, c_ctx, mod_w, mod_b, ln_g, ln_b, attn_wqkv, attn_wo, attn_lam, attn_subln,
              pool_w, pool_scale, fnet_w, fnet_b, hy_win, hy_bin, hy_conv_w, hy_conv_b, hy_f1, hy_fb1,
              hy_f2, hy_fb2, hy_f3, hy_freq, hy_skip, hy_wo, hy_bo, router_w, router_b, moe_wg, moe_wu, moe_wd):
    B, S, D = x.shape
    rope = axial_rope_tables(S)
    xl, xc = x, ctx
    for i in range(DEPTH):
        kind, j = i % N_MIXERS, i // N_MIXERS
        ctx_in, ctx_out = ctx_needed(i), ctx_needed(i + 1)
        ml = (jax.nn.silu(c) @ mod_w[i] + mod_b[i])[:, None, :]
        sh1, sc1, g1, sh2, sc2, g2 = jnp.split(ml, 6, axis=-1)
        hl = xl * (1.0 + sc1) + sh1
        if ctx_in:
            mc = jax.nn.silu(c_ctx) @ mod_w[i] + mod_b[i]
            csh1, csc1, cg1, csh2, csc2, cg2 = jnp.split(mc, 6)
            hc = xc * (1.0 + csc1) + csh1
        yc = None
        if kind == 0:
            lam_init = 0.8 - 0.6 * math.exp(-0.3 * i)
            ql, kl, vl = attn_qkv(hl, attn_wqkv[j])
            ql = apply_axial_rope(ql, *rope)
            kl = apply_axial_rope(kl, *rope)
            qc, kc, vc = attn_qkv(hc, attn_wqkv[j])
            yl = diff_attention(ql, jnp.concatenate([kl, kc], axis=1), jnp.concatenate([vl, vc], axis=1),
                                attn_lam[j], attn_subln[j], attn_wo[j], lam_init)
            if ctx_out:
                yc = diff_attention(qc, kc, vc, attn_lam[j], attn_subln[j], attn_wo[j], lam_init)
        elif kind == 1:
            yl = pool_mixer(hl, pool_w[j], pool_scale[j])
            if ctx_out:
                yc = pool_mixer(hc, pool_w[j], pool_scale[j])
        elif kind == 2:
            yl = fourier_mixer(hl, fnet_w[j], fnet_b[j])
            if ctx_out:
                yc = fourier_mixer(hc, fnet_w[j], fnet_b[j])
        else:
            hp = (hy_win[j], hy_bin[j], hy_conv_w[j], hy_conv_b[j], hy_f1[j], hy_fb1[j], hy_f2[j], hy_fb2[j],
                  hy_f3[j], hy_freq[j], hy_skip[j], hy_wo[j], hy_bo[j])
            yl = hyena_mixer(hl, *hp)
            if ctx_out:
                yc = hyena_mixer(hc, *hp)
        xl = deepnorm_update(xl, yl, g1, ln_g[i, 0], ln_b[i, 0])
        tokens = (xl * (1.0 + sc2) + sh2).reshape(B * S, D)
        if ctx_out:
            xc = deepnorm_update(xc, yc, cg1, ln_g[i, 0], ln_b[i, 0])
            tokens = jnp.concatenate([tokens, (xc * (1.0 + csc2) + csh2).reshape(-1, D)], axis=0)
        y2 = moe_ffn(tokens, router_w, router_b, moe_wg[i], moe_wu[i], moe_wd[i])
        xl = deepnorm_update(xl, y2[:B * S].reshape(B, S, D), g2, ln_g[i, 1], ln_b[i, 1])
        if ctx_out:
            xc = deepnorm_update(xc, y2[B * S:].reshape(B, -1, D), cg2, ln_g[i, 1], ln_b[i, 1])
    return xl
```

```python
import functools
import math

import numpy as np
import jax
import jax.numpy as jnp
from jax import lax
from jax.experimental import pallas as pl
from jax.experimental.pallas import tpu as pltpu

F32 = jnp.float32
BF16 = jnp.bfloat16
I32 = jnp.int32

GRID_W = 64
ATT_HEADS = 8
ROPE_THETA = 10000.0
POOL_WINDOWS = (2, 4, 8, 16)
FFT_GROUPS = 4
HY_BANDS = 16
HY_TARGET = 1e-2
HY_FAST = 0.3
HY_SLOW = 1.5
N_GROUPS = 4
EXPERTS_PER_GROUP = 4
LN_EPS = 1e-5

LANES = 128
SUBLANES = 8
MXU_DIM = 256
VMEM_LIMIT = 56 * 1024 * 1024

ROW_TILE = 512
MOE_BLOCK = 256
N_PAIR_CLASSES = 24
CLASS_ROWS = 32
ATT_TQ = 256
ATT_TK = 512
HALO = 16


def _cparams(sem):
    return pltpu.CompilerParams(dimension_semantics=sem, vmem_limit_bytes=VMEM_LIMIT)


def _mm(a, b):
    return jnp.dot(a, b, preferred_element_type=F32)


def _mm_nt(a, b):
    return lax.dot_general(a, b, (((1,), (1,)), ((), ())), preferred_element_type=F32)


def _silu(x):
    return x * jax.nn.sigmoid(x)


def _layer_norm(u, g, b):
    mu = jnp.mean(u, axis=-1, keepdims=True)
    d = u - mu
    var = jnp.mean(d * d, axis=-1, keepdims=True)
    return d * lax.rsqrt(var + LN_EPS) * g + b


def _mod_kernel(c_ref, w_ref, b_ref, o_ref):
    s = _silu(c_ref[...]).astype(BF16)
    o_ref[0] = _mm(s, w_ref[0].astype(BF16)) + b_ref[0]


def _mod_call(c8, mod_w, mod_b):
    depth, d, n = mod_w.shape
    tn = 1536
    return pl.pallas_call(
        _mod_kernel,
        out_shape=jax.ShapeDtypeStruct((depth, 8, n), F32),
        grid=(depth, n // tn),
        in_specs=[
            pl.BlockSpec((8, d), lambda i, j: (0, 0)),
            pl.BlockSpec((1, d, tn), lambda i, j: (i, 0, j)),
            pl.BlockSpec((1, 1, tn), lambda i, j: (i, 0, j)),
        ],
        out_specs=pl.BlockSpec((1, 8, tn), lambda i, j: (i, 0, j)),
        compiler_params=_cparams(("arbitrary", "arbitrary")),
        name="mod",
    )(c8, mod_w, mod_b.reshape(depth, 1, n))


def _route_rows(logits_t):
    m = jnp.max(logits_t, axis=0, keepdims=True)
    e = jnp.exp(logits_t - m)
    p = e / jnp.sum(e, axis=0, keepdims=True)
    r = [p[i:i + 1, :] for i in range(N_GROUPS * EXPERTS_PER_GROUP)]
    gs = []
    for g in range(N_GROUPS):
        q = r[4 * g:4 * g + 4]
        best = q[0] + q[1]
        for (i, j) in ((0, 2), (0, 3), (1, 2), (1, 3), (2, 3)):
            best = jnp.maximum(best, q[i] + q[j])
        gs.append(best)
    gbest, gidx = gs[0], jnp.zeros_like(gs[0], dtype=I32)
    for g in range(1, N_GROUPS):
        upd = gs[g] > gbest
        gidx = jnp.where(upd, g, gidx)
        gbest = jnp.where(upd, gs[g], gbest)
    v = []
    for j in range(EXPERTS_PER_GROUP):
        vj = r[j]
        for g in range(1, N_GROUPS):
            vj = jnp.where(gidx == g, r[4 * g + j], vj)
        v.append(vj)

    def first_argmax(vals):
        best, idx = vals[0], jnp.zeros_like(gidx)
        for j in range(1, len(vals)):
            upd = vals[j] > best
            idx = jnp.where(upd, j, idx)
            best = jnp.where(upd, vals[j], best)
        return best, idx

    v1, i1 = first_argmax(v)
    v2, i2 = first_argmax([jnp.where(i1 == j, -1.0, v[j]) for j in range(EXPERTS_PER_GROUP)])
    tot = v1 + v2
    w1, w2 = v1 / tot, v2 / tot
    lo = jnp.minimum(i1, i2)
    hi = jnp.maximum(i1, i2)
    pair = jnp.where(lo == 0, hi - 1, jnp.where(lo == 1, hi + 1, 5))
    cls = gidx * 6 + pair
    first_is_lo = i1 < i2
    w_lo = jnp.where(first_is_lo, w1, w2)
    w_hi = jnp.where(first_is_lo, w2, w1)
    return cls, w_lo, w_hi


def _post_kernel(a_ref, w_ref, bias_ref, scale_ref, x_ref, g1_ref, lng_ref, lnb_ref, sc2_ref, sh2_ref,
                 rwt_ref, rb_ref, tri_ref, x1_ref, tokx_ref, ri_ref, cnt_ref, cnt_scr, *, alpha, d):
    first = jnp.logical_and(pl.program_id(0) == 0, pl.program_id(1) == 0)

    @pl.when(first)
    def _():
        cnt_scr[...] = jnp.zeros_like(cnt_scr)

    y = _mm(a_ref[...].astype(BF16), w_ref[...])
    y = (y + bias_ref[...]) * scale_ref[...]
    u = alpha * x_ref[...] + (1.0 + g1_ref[...]) * y
    x1 = _layer_norm(u, lng_ref[...], lnb_ref[...])
    x1_ref[...] = x1
    tok = x1 * (1.0 + sc2_ref[...]) + sh2_ref[...]
    tokx_ref[:, :d] = tok
    tm = tok.shape[0]
    logits_t = _mm_nt(rwt_ref[...], tok.astype(BF16)) + rb_ref[...]
    cls, w_lo, w_hi = _route_rows(logits_t)
    crow = lax.broadcasted_iota(I32, (CLASS_ROWS, tm), 0)
    onehot = (crow == cls).astype(F32)
    prefix = _mm(onehot.astype(BF16), tri_ref[...])
    base = cnt_scr[...]
    rank = jnp.sum(onehot * (prefix + base), axis=0, keepdims=True).astype(I32)
    cnt_new = base + jnp.sum(onehot, axis=1, keepdims=True)
    cnt_scr[...] = cnt_new
    cnt_ref[...] = cnt_new[:, :LANES]
    r8 = lax.broadcasted_iota(I32, (SUBLANES, tm), 0)
    ri_ref[...] = jnp.where(r8 == 0, cls, jnp.where(r8 == 1, rank, 0))
    r128 = lax.broadcasted_iota(I32, (LANES, tm), 0)
    wmat = jnp.where(r128 == 0, w_lo, jnp.where(r128 == 1, w_hi, 0.0))
    tokx_ref[:, d:] = wmat.T


def _post_call(a, w, bias, scale, x, g1, lng, lnb, sc2, sh2, rwt, rb, alpha):
    bsz, l, d = x.shape
    k = a.shape[-1]
    tm = min(ROW_TILE, l)
    nt = l // tm
    t = bsz * l
    tri = jnp.asarray(np.triu(np.ones((tm, tm), np.float32), 1), BF16)
    vec = lambda: pl.BlockSpec((1, d), lambda b, i: (0, 0))
    bvec = lambda: pl.BlockSpec((None, 1, d), lambda b, i: (b, 0, 0))
    return pl.pallas_call(
        functools.partial(_post_kernel, alpha=alpha, d=d),
        out_shape=(
            jax.ShapeDtypeStruct((bsz, l, d), F32),
            jax.ShapeDtypeStruct((t, d + LANES), F32),
            jax.ShapeDtypeStruct((SUBLANES, t), I32),
            jax.ShapeDtypeStruct((CLASS_ROWS, LANES), F32),
        ),
        grid=(bsz, nt),
        in_specs=[
            pl.BlockSpec((None, tm, k), lambda b, i: (b, i, 0)),
            pl.BlockSpec((k, d), lambda b, i: (0, 0)),
            vec(), vec(),
            pl.BlockSpec((None, tm, d), lambda b, i: (b, i, 0)),
            bvec(), vec(), vec(), bvec(), bvec(),
            pl.BlockSpec(rwt.shape, lambda b, i: (0, 0)),
            pl.BlockSpec(rb.shape, lambda b, i: (0, 0)),
            pl.BlockSpec((tm, tm), lambda b, i: (0, 0)),
        ],
        out_specs=(
            pl.BlockSpec((None, tm, d), lambda b, i: (b, i, 0)),
            pl.BlockSpec((tm, d + LANES), lambda b, i: (b * nt + i, 0)),
            pl.BlockSpec((SUBLANES, tm), lambda b, i: (0, b * nt + i)),
            pl.BlockSpec((CLASS_ROWS, LANES), lambda b, i: (0, 0)),
        ),
        scratch_shapes=[pltpu.VMEM((CLASS_ROWS, tm), F32)],
        compiler_params=_cparams(("arbitrary", "arbitrary")),
        name="post",
    )(a, w, bias, scale, x, g1, lng, lnb, sc2, sh2, rwt, rb, tri)


def _dispatch_kernel(dest_ref, tok_hbm, xs_in_hbm, xs_hbm, sem, *, td):
    del xs_in_hbm
    base = pl.program_id(0) * td

    def row_copy(r):
        return pltpu.make_async_copy(tok_hbm.at[pl.ds(base + r, 1)],
                                     xs_hbm.at[pl.ds(dest_ref[base + r], 1)], sem)

    def start(r, c):
        row_copy(r).start()
        return c

    def wait(r, c):
        row_copy(r).wait()
        return c

    lax.fori_loop(0, td, start, 0)
    lax.fori_loop(0, td, wait, 0)


def _dispatch_call(dest, tokx, n_rows):
    t, dx = tokx.shape
    td = min(512, t)
    xs0 = jnp.zeros((n_rows, dx), F32)
    return pl.pallas_call(
        functools.partial(_dispatch_kernel, td=td),
        out_shape=jax.ShapeDtypeStruct((n_rows, dx), F32),
        grid_spec=pltpu.PrefetchScalarGridSpec(
            num_scalar_prefetch=1,
            grid=(t // td,),
            in_specs=[pl.BlockSpec(memory_space=pl.ANY), pl.BlockSpec(memory_space=pl.ANY)],
            out_specs=pl.BlockSpec(memory_space=pl.ANY),
            scratch_shapes=[pltpu.SemaphoreType.DMA],
        ),
        input_output_aliases={2: 0},
        compiler_params=_cparams(("arbitrary",)),
        name="dispatch",
    )(dest, tokx, xs0)


def _expert_kernel(ea_ref, eb_ref, nu_ref, xs_ref, wga_ref, wua_ref, wda_ref, wgb_ref, wub_ref, wdb_ref,
                   ys_ref, *, d, fc):
    del ea_ref, eb_ref

    @pl.when(pl.program_id(0) < nu_ref[0])
    def _():
        x = xs_ref[:, :d].astype(BF16)
        wts = xs_ref[:, d:]
        w_lo = wts[:, 0:1]
        w_hi = wts[:, 1:2]

        def ffn(wg_ref, wu_ref, wd_ref):
            f = wg_ref.shape[1]
            acc = None
            for c in range(f // fc):
                g = _mm(x, wg_ref[:, c * fc:(c + 1) * fc])
                u = _mm(x, wu_ref[:, c * fc:(c + 1) * fc])
                h = (_silu(g) * u).astype(BF16)
                part = _mm(h, wd_ref[c * fc:(c + 1) * fc, :])
                acc = part if acc is None else acc + part
            return acc

        ys_ref[...] = w_lo * ffn(wga_ref, wua_ref, wda_ref) + w_hi * ffn(wgb_ref, wub_ref, wdb_ref)

    @pl.when(pl.program_id(0) >= nu_ref[0])
    def _():
        ys_ref[...] = jnp.zeros_like(ys_ref)


def _expert_call(blk_ea, blk_eb, n_used, xs, wg, wu, wd):
    n_rows, dx = xs.shape
    d = dx - LANES
    f = wg.shape[-1]
    bm = MOE_BLOCK
    n_blocks = n_rows // bm
    xmap = lambda b, ea, eb, nu: (jnp.minimum(b, nu[0] - 1), 0)
    amap = lambda b, ea, eb, nu: (ea[b], 0, 0)
    bmap = lambda b, ea, eb, nu: (eb[b], 0, 0)
    return pl.pallas_call(
        functools.partial(_expert_kernel, d=d, fc=512),
        out_shape=jax.ShapeDtypeStruct((n_rows, d), F32),
        grid_spec=pltpu.PrefetchScalarGridSpec(
            num_scalar_prefetch=3,
            grid=(n_blocks,),
            in_specs=[
                pl.BlockSpec((bm, dx), xmap),
                pl.BlockSpec((None, d, f), amap), pl.BlockSpec((None, d, f), amap), pl.BlockSpec((None, f, d), amap),
                pl.BlockSpec((None, d, f), bmap), pl.BlockSpec((None, d, f), bmap), pl.BlockSpec((None, f, d), bmap),
            ],
            out_specs=pl.BlockSpec((bm, d), lambda b, ea, eb, nu: (b, 0)),
        ),
        compiler_params=_cparams(("arbitrary",)),
        name="experts",
    )(blk_ea, blk_eb, n_used, xs, wg, wu, wd, wg, wu, wd)


def _combine_kernel(dest_ref, ys_hbm, x1_ref, g2_ref, lng_ref, lnb_ref, o_ref, buf, sem, *, tm, alpha):
    i = pl.program_id(0) * pl.num_programs(1) + pl.program_id(1)
    n = pl.num_programs(0) * pl.num_programs(1)
    slot = i % 2

    def row_copy(step, s, r):
        return pltpu.make_async_copy(ys_hbm.at[pl.ds(dest_ref[step * tm + r], 1)],
                                     buf.at[s, pl.ds(r, 1)], sem.at[s])

    def issue(step, s):
        def body(r, c):
            row_copy(step, s, r).start()
            return c
        lax.fori_loop(0, tm, body, 0)

    @pl.when(i == 0)
    def _():
        issue(0, 0)

    @pl.when(i + 1 < n)
    def _():
        issue(i + 1, 1 - slot)

    def wait(r, c):
        row_copy(i, slot, r).wait()
        return c

    lax.fori_loop(0, tm, wait, 0)
    u = alpha * x1_ref[...] + (1.0 + g2_ref[...]) * buf[slot]
    o_ref[...] = _layer_norm(u, lng_ref[...], lnb_ref[...])


def _combine_call(dest, ys, x1, g2, lng, lnb, alpha):
    bsz, l, d = x1.shape
    tm = min(256, l)
    nt = l // tm
    return pl.pallas_call(
        functools.partial(_combine_kernel, tm=tm, alpha=alpha),
        out_shape=jax.ShapeDtypeStruct((bsz, l, d), F32),
        grid_spec=pltpu.PrefetchScalarGridSpec(
            num_scalar_prefetch=1,
            grid=(bsz, nt),
            in_specs=[
                pl.BlockSpec(memory_space=pl.ANY),
                pl.BlockSpec((None, tm, d), lambda b, i, dst: (b, i, 0)),
                pl.BlockSpec((None, 1, d), lambda b, i, dst: (b, 0, 0)),
                pl.BlockSpec((1, d), lambda b, i, dst: (0, 0)),
                pl.BlockSpec((1, d), lambda b, i, dst: (0, 0)),
            ],
            out_specs=pl.BlockSpec((None, tm, d), lambda b, i, dst: (b, i, 0)),
            scratch_shapes=[pltpu.VMEM((2, tm, d), F32), pltpu.SemaphoreType.DMA((2,))],
        ),
        compiler_params=_cparams(("arbitrary", "arbitrary")),
        name="combine",
    )(dest, ys, x1, g2, lng, lnb)


def _pair_tables():
    ta, tb = [], []
    for g in range(N_GROUPS):
        for a in range(EXPERTS_PER_GROUP):
            for b in range(a + 1, EXPERTS_PER_GROUP):
                ta.append(g * EXPERTS_PER_GROUP + a)
                tb.append(g * EXPERTS_PER_GROUP + b)
    return np.asarray(ta, np.int32), np.asarray(tb, np.int32)


def _moe(x1, tokx, ri, cnt, g2, lng, lnb, wg, wu, wd, alpha):
    bsz, l, d = x1.shape
    t = bsz * l
    bm = MOE_BLOCK
    n_blocks = t // bm + N_PAIR_CLASSES
    counts = cnt[:N_PAIR_CLASSES, 0].astype(I32)
    nblk = (counts + bm - 1) // bm
    cend = jnp.cumsum(nblk)
    cstart = cend - nblk
    dest = cstart[ri[0]] * bm + ri[1]
    blk_cls = jnp.minimum(jnp.searchsorted(cend, jnp.arange(n_blocks, dtype=I32), side="right"),
                          N_PAIR_CLASSES - 1)
    ta, tb = _pair_tables()
    blk_ea = jnp.asarray(ta)[blk_cls]
    blk_eb = jnp.asarray(tb)[blk_cls]
    n_used = cend[-1:].astype(I32)
    xs = _dispatch_call(dest, tokx, n_blocks * bm)
    ys = _expert_call(blk_ea, blk_eb, n_used, xs, wg, wu, wd)
    return _combine_call(dest, ys, x1, g2, lng, lnb, alpha)


def _qkv_kernel(x_ref, sc_ref, sh_ref, w_ref, *rest, rope, groups, d, q_scale):
    if rope:
        tab_ref, out_refs = rest[0], rest[1:]
    else:
        tab_ref, out_refs = None, rest
    h = (x_ref[...] * (1.0 + sc_ref[...]) + sh_ref[...]).astype(BF16)
    for (g, transposed), o_ref in zip(groups, out_refs):
        for c in range(d // MXU_DIM):
            col = g * d + c * MXU_DIM
            z = _mm(h, w_ref[:, col:col + MXU_DIM])
            if rope and g < 2:
                heads = []
                for hh in range(MXU_DIM // LANES):
                    zh = z[:, hh * LANES:(hh + 1) * LANES]
                    heads.append(zh * tab_ref[0] + pltpu.roll(zh, LANES - 16, 1) * tab_ref[1]
                                 + pltpu.roll(zh, 16, 1) * tab_ref[2])
                z = jnp.concatenate(heads, axis=1)
            if g == 0:
                z = z * q_scale
            if transposed:
                o_ref[c * MXU_DIM:(c + 1) * MXU_DIM, :] = z.T.astype(BF16)
            else:
                o_ref[:, c * MXU_DIM:(c + 1) * MXU_DIM] = z.astype(BF16)


def _qkv_call(x, sc, sh, w, tab, groups, q_scale):
    bsz, l, d = x.shape
    tm = min(ROW_TILE, l)
    rope = tab is not None
    in_specs = [
        pl.BlockSpec((None, tm, d), lambda b, i: (b, i, 0)),
        pl.BlockSpec((None, 1, d), lambda b, i: (b, 0, 0)),
        pl.BlockSpec((None, 1, d), lambda b, i: (b, 0, 0)),
        pl.BlockSpec(w.shape, lambda b, i: (0, 0)),
    ]
    args = [x, sc, sh, w]
    if rope:
        in_specs.append(pl.BlockSpec((3, tm, LANES), lambda b, i: (0, i, 0)))
        args.append(tab)
    out_shape, out_specs = [], []
    for (_, transposed) in groups:
        if transposed:
            out_shape.append(jax.ShapeDtypeStruct((bsz, d, l), BF16))
            out_specs.append(pl.BlockSpec((None, d, tm), lambda b, i: (b, 0, i)))
        else:
            out_shape.append(jax.ShapeDtypeStruct((bsz, l, d), BF16))
            out_specs.append(pl.BlockSpec((None, tm, d), lambda b, i: (b, i, 0)))
    return pl.pallas_call(
        functools.partial(_qkv_kernel, rope=rope, groups=groups, d=d, q_scale=q_scale),
        out_shape=tuple(out_shape),
        grid=(bsz, l // tm),
        in_specs=in_specs,
        out_specs=tuple(out_specs),
        compiler_params=_cparams(("arbitrary", "arbitrary")),
        name="qkv",
    )(*args)


def _attn_kernel(qt_ref, k_ref, vt_ref, kc_ref, vct_ref, lam_ref, subln_ref, o_ref, m_scr, l_scr, acc_scr,
                 *, tq, tk, lam_init):
    qt = qt_ref[...]
    half = qt.shape[0] // 2
    row = lax.broadcasted_iota(I32, qt.shape, 0)
    zero = jnp.zeros_like(qt)
    rhs = jnp.concatenate([jnp.where(row < half, qt, zero), jnp.where(row >= half, qt, zero)], axis=1)
    m_scr[...] = jnp.full_like(m_scr, -jnp.inf)
    l_scr[...] = jnp.zeros_like(l_scr)
    acc_scr[...] = jnp.zeros_like(acc_scr)

    def step(kb, vtb):
        s = _mm(kb, rhs)
        m_old = m_scr[...]
        m_new = jnp.maximum(m_old, jnp.max(s, axis=0, keepdims=True))
        a = jnp.exp(m_old - m_new)
        p = jnp.exp(s - m_new)
        l_scr[...] = a * l_scr[...] + jnp.sum(p, axis=0, keepdims=True)
        acc_scr[...] = a * acc_scr[...] + _mm(vtb, p.astype(BF16))
        m_scr[...] = m_new

    def body(j, c):
        off = pl.multiple_of(j * tk, tk)
        step(k_ref[pl.ds(off, tk), :], vt_ref[:, pl.ds(off, tk)])
        return c

    lax.fori_loop(0, k_ref.shape[0] // tk, body, 0)
    step(kc_ref[...], vct_ref[...])

    lam = lam_ref[...]
    lam_full = (jnp.exp(jnp.sum(lam[0:1] * lam[1:2], axis=1, keepdims=True))
                - jnp.exp(jnp.sum(lam[2:3] * lam[3:4], axis=1, keepdims=True)) + lam_init)
    o12 = acc_scr[...] / l_scr[...]
    o = o12[:, :tq] - lam_full * o12[:, tq:]
    o = o * lax.rsqrt(jnp.mean(o * o, axis=0, keepdims=True) + LN_EPS)
    o = o * (subln_ref[...] * (1.0 - lam_init))
    o_ref[...] = o.T.astype(BF16)


def _attn_call(qt, k, vt, kc, vct, lam, subln_col, lam_init):
    bsz, d, l = qt.shape
    lc = kc.shape[1]
    hd = d // ATT_HEADS
    tq = min(ATT_TQ, l)
    tk = min(ATT_TK, l)
    return pl.pallas_call(
        functools.partial(_attn_kernel, tq=tq, tk=tk, lam_init=lam_init),
        out_shape=jax.ShapeDtypeStruct((bsz, l, d), BF16),
        grid=(bsz, ATT_HEADS, l // tq),
        in_specs=[
            pl.BlockSpec((None, hd, tq), lambda b, h, i: (b, h, i)),
            pl.BlockSpec((None, l, hd), lambda b, h, i: (b, 0, h)),
            pl.BlockSpec((None, hd, l), lambda b, h, i: (b, h, 0)),
            pl.BlockSpec((None, lc, hd), lambda b, h, i: (b, 0, h)),
            pl.BlockSpec((None, hd, lc), lambda b, h, i: (b, h, 0)),
            pl.BlockSpec(lam.shape, lambda b, h, i: (0, 0)),
            pl.BlockSpec(subln_col.shape, lambda b, h, i: (0, 0)),
        ],
        out_specs=pl.BlockSpec((None, tq, hd), lambda b, h, i: (b, i, h)),
        scratch_shapes=[pltpu.VMEM((1, 2 * tq), F32), pltpu.VMEM((1, 2 * tq), F32),
                        pltpu.VMEM((hd, 2 * tq), F32)],
        compiler_params=_cparams(("arbitrary", "arbitrary", "arbitrary")),
        name="diff_attn",
    )(qt, k, vt, kc, vct, lam, subln_col)


def _rope_tables(l, hd):
    dh = hd // 2
    quarter = dh // 4
    pos = np.arange(l)
    row = (pos // GRID_W).astype(np.float32)
    col = (pos % GRID_W).astype(np.float32)
    inv = (ROPE_THETA ** (-np.arange(0, dh // 2, 2, dtype=np.float32) / (dh // 2))).astype(np.float32)
    ar = row[:, None] * inv
    ac = col[:, None] * inv
    ang = np.concatenate([ar, ar, ac, ac], axis=-1).astype(np.float32)
    cos, sin = np.cos(ang), np.sin(ang)
    lane = np.arange(dh)
    first = (lane % (2 * quarter)) < quarter
    sin_a = np.where(first, -sin, 0.0)
    sin_b = np.where(first, 0.0, sin)
    tab = np.stack([np.tile(cos, (1, 2)), np.tile(sin_a, (1, 2)), np.tile(sin_b, (1, 2))]).astype(np.float32)
    return jnp.asarray(tab)


def _attention_mixer(x, ctx, sc1, sh1, csc1, csh1, wqkv, lam, subln, layer_idx):
    bsz, l, d = x.shape
    hd = d // ATT_HEADS
    lam_init = 0.8 - 0.6 * math.exp(-0.3 * layer_idx)
    w = wqkv.astype(BF16)
    qt, k, vt = _qkv_call(x, sc1, sh1, w, _rope_tables(l, hd), ((0, True), (1, False), (2, True)),
                          (hd // 2) ** -0.5)
    ones = jnp.ones((bsz, 1, 1), F32)
    kc, vct = _qkv_call(ctx, csc1[None, None, :] * ones, csh1[None, None, :] * ones, w, None,
                        ((1, False), (2, True)), 1.0)
    subln_col = subln.reshape(hd, 1)
    return _attn_call(qt, k, vt, kc, vct, lam, subln_col, lam_init)


def _halo_specs(tm, d, l):
    per = tm // HALO
    last = l // HALO - 1
    return [
        pl.BlockSpec((None, tm, d), lambda b, i: (b, i, 0)),
        pl.BlockSpec((None, HALO, d), lambda b, i: (b, jnp.maximum(i * per - 1, 0), 0)),
        pl.BlockSpec((None, HALO, d), lambda b, i: (b, jnp.minimum((i + 1) * per, last), 0)),
    ]


def _pool_kernel(x_ref, xp_ref, xn_ref, sc_ref, sh_ref, o_ref, ext, *, tm, l):
    i = pl.program_id(1)
    nt = pl.num_programs(1)
    sc = 1.0 + sc_ref[...]
    sh = sh_ref[...]
    ext[HALO:HALO + tm, :] = x_ref[...] * sc + sh
    ext[0:HALO, :] = jnp.where(i > 0, xp_ref[...] * sc + sh, 0.0)
    ext[HALO + tm:, :] = jnp.where(i < nt - 1, xn_ref[...] * sc + sh, 0.0)
    gw = x_ref.shape[1] // len(POOL_WINDOWS)
    t = i * tm + lax.broadcasted_iota(I32, (tm, gw), 0)
    for g, win in enumerate(POOL_WINDOWS):
        c0 = g * gw
        acc = None
        for j in range(-(win // 2), win - win // 2):
            part = ext[HALO + j:HALO + j + tm, c0:c0 + gw]
            acc = part if acc is None else acc + part
        lo = jnp.maximum(t - win // 2, 0)
        hi = jnp.minimum(t - win // 2 + win, l)
        mean = acc / (hi - lo).astype(F32)
        o_ref[:, c0:c0 + gw] = (mean - ext[HALO:HALO + tm, c0:c0 + gw]).astype(BF16)


def _pool_call(x, sc1, sh1):
    bsz, l, d = x.shape
    tm = min(ROW_TILE, l)
    bvec = lambda: pl.BlockSpec((None, 1, d), lambda b, i: (b, 0, 0))
    return pl.pallas_call(
        functools.partial(_pool_kernel, tm=tm, l=l),
        out_shape=jax.ShapeDtypeStruct((bsz, l, d), BF16),
        grid=(bsz, l // tm),
        in_specs=_halo_specs(tm, d, l) + [bvec(), bvec()],
        out_specs=pl.BlockSpec((None, tm, d), lambda b, i: (b, i, 0)),
        scratch_shapes=[pltpu.VMEM((tm + 2 * HALO, d), F32)],
        compiler_params=_cparams(("arbitrary", "arbitrary")),
        name="pool",
    )(x, x, x, sc1, sh1)


KRON = SUBLANES
INNER = 128
N_RB = INNER // KRON


def _kron_rows(m):
    p, q = m.shape
    return np.einsum("pq,jk->pjqk", m, np.eye(KRON)).reshape(p * KRON, q * KRON)


def _outer_kernel(*refs, n_in):
    m_ref, x_refs, o_ref = refs[0], refs[1:1 + n_in], refs[1 + n_in]
    acc = None
    col = 0
    for x_ref in x_refs:
        a, j, d = x_ref.shape
        x = x_ref[...].reshape(a * j, d).astype(BF16)
        part = _mm(m_ref[:, col:col + a * j], x)
        acc = part if acc is None else acc + part
        col += a * j
    o_ref[...] = acc.astype(o_ref.dtype)


def _outer_fwd_call(m, xs, a_out):
    bsz, rows, d = xs[0].shape
    a_in = rows // INNER
    xs4 = [x.reshape(bsz, a_in, N_RB, KRON, d) for x in xs]
    return pl.pallas_call(
        functools.partial(_outer_kernel, n_in=len(xs)),
        out_shape=jax.ShapeDtypeStruct((bsz, N_RB, a_out * 2 * KRON, d), BF16),
        grid=(bsz, N_RB),
        in_specs=[pl.BlockSpec(m.shape, lambda b, r: (0, 0))]
        + [pl.BlockSpec((None, a_in, None, KRON, d), lambda b, r: (b, 0, r, 0, 0)) for _ in xs],
        out_specs=pl.BlockSpec((None, None, a_out * 2 * KRON, d), lambda b, r: (b, r, 0, 0)),
        compiler_params=_cparams(("arbitrary", "arbitrary")),
        name="dft_outer",
    )(m, *xs4)


def _fn_chan_kernel(x_ref, sc_ref, sh_ref, cs_ref, pr_ref, pi_ref, *, gw):
    h = (x_ref[...] * (1.0 + sc_ref[...]) + sh_ref[...]).astype(BF16)
    for g in range(h.shape[1] // gw):
        z = _mm(h[:, g * gw:(g + 1) * gw], cs_ref[...])
        pr_ref[:, g * gw:(g + 1) * gw] = z[:, :gw]
        pi_ref[:, g * gw:(g + 1) * gw] = z[:, gw:]


def _fn_chan_call(x, sc1, sh1):
    bsz, l, d = x.shape
    gw = d // FFT_GROUPS
    tm = min(ROW_TILE, l)
    ang = 2.0 * np.pi * np.outer(np.arange(gw), np.arange(gw)) / gw
    cs = jnp.asarray(np.concatenate([np.cos(ang), -np.sin(ang)], axis=1), BF16)
    bvec = lambda: pl.BlockSpec((None, 1, d), lambda b, i: (b, 0, 0))
    tile = lambda: pl.BlockSpec((None, tm, d), lambda b, i: (b, i, 0))
    return pl.pallas_call(
        functools.partial(_fn_chan_kernel, gw=gw),
        out_shape=(jax.ShapeDtypeStruct((bsz, l, d), F32), jax.ShapeDtypeStruct((bsz, l, d), F32)),
        grid=(bsz, l // tm),
        in_specs=[tile(), bvec(), bvec(), pl.BlockSpec(cs.shape, lambda b, i: (0, 0))],
        out_specs=(tile(), tile()),
        compiler_params=_cparams(("arbitrary", "arbitrary")),
        name="fnet_channel_dft",
    )(x, sc1, sh1, cs)


def _fn_inner_kernel(m_ref, x_ref, o_ref):
    rb, j1, cj, d = x_ref.shape
    x = x_ref[...].reshape(rb * j1 * cj, d)
    y = _mm(m_ref[...], x)
    o_ref[...] = y.reshape(o_ref.shape)


def _fn_inner_call(m2, spec, a):
    bsz, _, _, d = spec.shape
    spec5 = spec.reshape(bsz, N_RB, a, 2 * KRON, d)
    out = pl.pallas_call(
        _fn_inner_kernel,
        out_shape=jax.ShapeDtypeStruct((bsz, INNER, a, d), F32),
        grid=(a // KRON, bsz),
        in_specs=[
            pl.BlockSpec((None,) + m2.shape[1:], lambda s, b: (s, 0, 0)),
            pl.BlockSpec((None, N_RB, KRON, 2 * KRON, d), lambda s, b: (b, 0, s, 0, 0)),
        ],
        out_specs=pl.BlockSpec((None, INNER, KRON, d), lambda s, b: (b, 0, s, 0)),
        compiler_params=_cparams(("arbitrary", "arbitrary")),
        name="fnet_inner_dft",
    )(m2, spec5)
    return out.reshape(bsz, INNER * a, d)


def _angle_sum(alpha, beta):
    ca, sa = jnp.asarray(np.cos(alpha), F32), jnp.asarray(np.sin(alpha), F32)
    cb, sb = jnp.asarray(np.cos(beta), F32), jnp.asarray(np.sin(beta), F32)
    return ca * cb - sa * sb, sa * cb + ca * sb


def _fnet_tables(l, gw):
    a = l // INNER
    k1 = np.arange(a)
    ang1 = 2.0 * np.pi * np.outer(k1, k1) / a
    c1, s1 = np.cos(ang1), np.sin(ang1)
    m1 = np.zeros((a, 2, 2, a))
    m1[:, 0, 0, :], m1[:, 0, 1, :] = c1, s1
    m1[:, 1, 0, :], m1[:, 1, 1, :] = -s1, c1
    m1k = np.einsum("pcea,jk->pcjeak", m1, np.eye(KRON)).reshape(a * 2 * KRON, 2 * a * KRON)
    l2 = np.arange(INNER)
    alpha = 2.0 * np.pi * np.outer(np.arange(INNER), l2) / INNER
    beta = 2.0 * np.pi * np.outer(k1, l2) / l
    cos_t, sin_t = _angle_sum(alpha[None, :, :], beta[:, None, :])
    norm = 1.0 / math.sqrt(l * gw)
    cs = jnp.stack([cos_t, sin_t], axis=2) * norm
    cs = cs.reshape(a // KRON, KRON, INNER, 2, N_RB, KRON)
    m2 = jnp.einsum("spkcrj,qp->skqrpcj", cs, jnp.eye(KRON, dtype=F32))
    m2 = m2.reshape(a // KRON, INNER * KRON, N_RB * KRON * 2 * KRON)
    return jnp.asarray(m1k, BF16), m2.astype(BF16)


def _fourier_mixer(x, sc1, sh1):
    bsz, l, d = x.shape
    a = l // INNER
    m1k, m2 = _fnet_tables(l, d // FFT_GROUPS)
    pr, pi = _fn_chan_call(x, sc1, sh1)
    spec = _outer_fwd_call(m1k, [pr, pi], a)
    return _fn_inner_call(m2, spec, a)


def _hy_front_kernel(x_ref, xp_ref, xn_ref, sc_ref, sh_ref, w_ref, bin_ref, cw_ref, cb_ref,
                     vg_ref, x0_ref, hext, zscr, *, tm, l, d):
    i = pl.program_id(1)
    sc = 1.0 + sc_ref[...]
    sh = sh_ref[...]
    hext[HALO:HALO + tm, :] = (x_ref[...] * sc + sh).astype(BF16)
    hext[0:HALO, :] = (xp_ref[...] * sc + sh).astype(BF16)
    hext[HALO + tm:, :] = (xn_ref[...] * sc + sh).astype(BF16)
    pos = i * tm - HALO + lax.broadcasted_iota(I32, (tm + 2 * HALO, MXU_DIM), 0)

    def conv(col):
        z = _mm(hext[...], w_ref[:, col:col + MXU_DIM]) + bin_ref[:, col:col + MXU_DIM]
        zscr[...] = jnp.where(pos >= 0, jnp.where(pos < l, z, 0.0), 0.0)
        cw = cw_ref[:, col:col + MXU_DIM]
        return (zscr[HALO - 1:HALO - 1 + tm, :] * cw[0:1] + zscr[HALO:HALO + tm, :] * cw[1:2]
                + zscr[HALO + 1:HALO + 1 + tm, :] * cw[2:3] + cb_ref[:, col:col + MXU_DIM])

    for c in range(d // MXU_DIM):
        cols = slice(c * MXU_DIM, (c + 1) * MXU_DIM)
        x0_ref[:, cols] = conv(c * MXU_DIM)
        x1 = conv(d + c * MXU_DIM)
        v = conv(2 * d + c * MXU_DIM)
        vg_ref[:, cols] = v * x1


def _hy_front_call(x, sc1, sh1, win, bin_, conv_w, conv_b):
    bsz, l, d = x.shape
    tm = min(ROW_TILE, l)
    n3 = win.shape[1]
    bvec = lambda: pl.BlockSpec((None, 1, d), lambda b, i: (b, 0, 0))
    full = lambda a: pl.BlockSpec(a.shape, lambda b, i: (0, 0))
    tile = lambda: pl.BlockSpec((None, tm, d), lambda b, i: (b, i, 0))
    args = (win.astype(BF16), bin_.reshape(1, n3), conv_w, conv_b.reshape(1, n3))
    return pl.pallas_call(
        functools.partial(_hy_front_kernel, tm=tm, l=l, d=d),
        out_shape=(jax.ShapeDtypeStruct((bsz, l, d), F32), jax.ShapeDtypeStruct((bsz, l, d), F32)),
        grid=(bsz, l // tm),
        in_specs=_halo_specs(tm, d, l) + [bvec(), bvec()] + [full(a) for a in args],
        out_specs=(tile(), tile()),
        scratch_shapes=[pltpu.VMEM((tm + 2 * HALO, d), BF16), pltpu.VMEM((tm + 2 * HALO, MXU_DIM), F32)],
        compiler_params=_cparams(("arbitrary", "arbitrary")),
        name="hyena_front",
    )(x, x, x, sc1, sh1, *args)


def _hy_filter_kernel(z_ref, f1_ref, fb1_ref, f2_ref, fb2_ref, f3_ref, fr_ref, del_ref, o_ref, *, tf, l):
    z = z_ref[...]
    fr = fr_ref[...]
    a = jnp.sin(fr * (_mm(z, f1_ref[...]) + fb1_ref[...]))
    a = jnp.sin(fr * (_mm(a, f2_ref[...]) + fb2_ref[...]))
    k = _mm(a, f3_ref[...])
    window = jnp.exp(-z[:, 0:1] * del_ref[...])
    row = pl.program_id(0) * tf + lax.broadcasted_iota(I32, k.shape, 0)
    o_ref[...] = jnp.where(row == l, 0.0, k * window)


def _hy_filter_call(l, d, f1, fb1, f2, fb2, f3, freq):
    n = 2 * l
    tf = min(ROW_TILE, l)
    hid = f1.shape[1]
    pos = jnp.concatenate([jnp.arange(l, dtype=F32), jnp.zeros((1,), F32),
                           jnp.arange(l - 1, 0, -1, dtype=F32)])
    bands = jnp.linspace(1e-4, HY_BANDS - 1, HY_BANDS, dtype=F32)
    ang = (2.0 * math.pi / l) * pos[:, None] * bands[None, :]
    feat = jnp.concatenate([(pos / l)[:, None], jnp.cos(ang), jnp.sin(ang)], axis=-1)
    kin = 64
    feat = jnp.pad(feat, ((0, 0), (0, kin - feat.shape[1])))
    f1p = jnp.pad(f1, ((0, kin - f1.shape[0]), (0, 0)))
    deltas = np.abs(np.linspace(math.log(HY_TARGET) / HY_SLOW, math.log(HY_TARGET) / HY_FAST, d,
                                dtype=np.float32)).reshape(1, d)
    half = l // tf
    full = lambda a: pl.BlockSpec(a.shape, lambda i: (0, 0))
    args = (f1p, fb1.reshape(1, hid), f2, fb2.reshape(1, hid))
    tail = (freq.reshape(1, hid), jnp.asarray(deltas))
    return pl.pallas_call(
        functools.partial(_hy_filter_kernel, tf=tf, l=l),
        out_shape=jax.ShapeDtypeStruct((n, d), F32),
        grid=(n // tf,),
        in_specs=[pl.BlockSpec((tf, kin), lambda i: (i, 0))] + [full(a) for a in args]
        + [pl.BlockSpec((hid, d), lambda i: (0, i // half))] + [full(a) for a in tail],
        out_specs=pl.BlockSpec((tf, d), lambda i: (i, 0)),
        compiler_params=_cparams(("arbitrary",)),
        name="hyena_filter",
    )(feat, *args, f3, *tail)


def _hy_kspec_kernel(mf_ref, x_ref, o_ref):
    rb, cj, d = x_ref.shape
    o_ref[...] = _mm(mf_ref[...], x_ref[...].reshape(rb * cj, d))


def _hy_kspec_call(mf2, spec, a):
    _, _, _, d = spec.shape
    spec5 = spec.reshape(1, N_RB, a, 2 * KRON, d)
    return pl.pallas_call(
        _hy_kspec_kernel,
        out_shape=jax.ShapeDtypeStruct((a, 2 * INNER, d), F32),
        grid=(a,),
        in_specs=[
            pl.BlockSpec((None, 2 * INNER, 2 * INNER), lambda k: (k, 0, 0)),
            pl.BlockSpec((None, N_RB, None, 2 * KRON, d), lambda k: (0, 0, k, 0, 0)),
        ],
        out_specs=pl.BlockSpec((None, 2 * INNER, d), lambda k: (k, 0, 0)),
        compiler_params=_cparams(("arbitrary",)),
        name="hyena_filter_spectrum",
    )(mf2, spec5)


def _hy_mid_kernel(mf_ref, mi_ref, k_ref, x_ref, o_ref):
    rb, cj, d = x_ref.shape
    v = _mm(mf_ref[...], x_ref[...].reshape(rb * cj, d))
    kk = k_ref[...]
    vr, vi = v[:INNER], v[INNER:]
    kr, ki = kk[:INNER], kk[INNER:]
    z = jnp.concatenate([vr * kr - vi * ki, vr * ki + vi * kr], axis=0).astype(BF16)
    y = _mm(mi_ref[...], z)
    o_ref[...] = y.astype(BF16).reshape(rb, cj, d)


def _hy_mid_call(mf2, mi1, kspec, spec, a):
    bsz, _, _, d = spec.shape
    spec5 = spec.reshape(bsz, N_RB, a, 2 * KRON, d)
    mat = lambda: pl.BlockSpec((None, 2 * INNER, 2 * INNER), lambda k, b: (k, 0, 0))
    blk = lambda: pl.BlockSpec((None, N_RB, None, 2 * KRON, d), lambda k, b: (b, 0, k, 0, 0))
    out = pl.pallas_call(
        _hy_mid_kernel,
        out_shape=jax.ShapeDtypeStruct(spec5.shape, BF16),
        grid=(a, bsz),
        in_specs=[mat(), mat(), pl.BlockSpec((None, 2 * INNER, d), lambda k, b: (k, 0, 0)), blk()],
        out_specs=blk(),
        compiler_params=_cparams(("arbitrary", "arbitrary")),
        name="hyena_spectral_product",
    )(mf2, mi1, kspec, spec5)
    return out.reshape(spec.shape)


def _hy_out_kernel(m_ref, b_ref, vg_ref, x0_ref, skip_ref, o_ref):
    a2, j, d = vg_ref.shape
    y = _mm(m_ref[...], b_ref[...])
    vg = vg_ref[...].reshape(a2 * j, d)
    x0 = x0_ref[...].reshape(a2 * j, d)
    o_ref[...] = ((y + vg * skip_ref[...]) * x0).reshape(a2, j, d)


def _hy_out_call(mi2, spec, vg, x0, skip):
    bsz, l, d = vg.shape
    a2 = l // INNER
    view = lambda t: t.reshape(bsz, a2, N_RB, KRON, d)
    blk = lambda: pl.BlockSpec((None, a2, None, KRON, d), lambda b, r: (b, 0, r, 0, 0))
    out = pl.pallas_call(
        _hy_out_kernel,
        out_shape=jax.ShapeDtypeStruct((bsz, a2, N_RB, KRON, d), F32),
        grid=(bsz, N_RB),
        in_specs=[
            pl.BlockSpec(mi2.shape, lambda b, r: (0, 0)),
            pl.BlockSpec((None, None) + spec.shape[2:], lambda b, r: (b, r, 0, 0)),
            blk(), blk(),
            pl.BlockSpec((1, d), lambda b, r: (0, 0)),
        ],
        out_specs=blk(),
        compiler_params=_cparams(("arbitrary", "arbitrary")),
        name="hyena_inverse_outer",
    )(mi2, spec, view(vg), view(x0), skip.reshape(1, d))
    return out.reshape(bsz, l, d)


def _hyena_tables(l):
    n = 2 * l
    a = n // INNER
    kb = np.arange(a)
    ang = 2.0 * np.pi * np.outer(kb, kb) / a
    fwd = np.stack([np.cos(ang), -np.sin(ang)], axis=1)
    mf1 = _kron_rows(fwd[:, :, :a // 2].reshape(2 * a, a // 2))
    mk1 = _kron_rows(fwd.reshape(2 * a, a))
    inv = np.stack([np.cos(ang), -np.sin(ang)], axis=2)[:a // 2] / n
    mi2 = _kron_rows(inv.reshape(a // 2, 2 * a))
    li = np.arange(INNER)
    alpha = 2.0 * np.pi * np.outer(np.arange(INNER), li) / INNER
    beta = 2.0 * np.pi * np.outer(kb, li) / n
    cos_t, sin_t = _angle_sum(alpha[None, :, :], beta[:, None, :])
    top = jnp.stack([cos_t, sin_t], axis=2)
    bot = jnp.stack([-sin_t, cos_t], axis=2)
    mf2 = jnp.stack([top, bot], axis=1)
    mf2 = mf2.reshape(a, 2, INNER, 2, N_RB, KRON).transpose(0, 1, 2, 4, 3, 5)
    mf2 = mf2.reshape(a, 2 * INNER, 2 * INNER).astype(BF16)
    mi1 = jnp.swapaxes(mf2, 1, 2)
    return (jnp.asarray(mf1, BF16), jnp.asarray(mk1, BF16), jnp.asarray(mi2, BF16), mf2, mi1)


def _hyena_mixer(x, sc1, sh1, win, bin_, conv_w, conv_b, f1, fb1, f2, fb2, f3, freq, skip):
    bsz, l, d = x.shape
    a = 2 * l // INNER
    mf1, mk1, mi2, mf2, mi1 = _hyena_tables(l)
    vg, x0 = _hy_front_call(x, sc1, sh1, win, bin_, conv_w, conv_b)
    kfull = _hy_filter_call(l, d, f1, fb1, f2, fb2, f3, freq)
    kspec = _hy_kspec_call(mf2, _outer_fwd_call(mk1, [kfull[None]], a), a)
    spec = _outer_fwd_call(mf1, [vg], a)
    spec = _hy_mid_call(mf2, mi1, kspec, spec, a)
    return _hy_out_call(mi2, spec, vg, x0, skip)


def kernel(x, c, ctx, c_ctx, mod_w, mod_b, ln_g, ln_b, attn_wqkv, attn_wo, attn_lam, attn_subln, pool_w,
           pool_scale, fnet_w, fnet_b, hy_win, hy_bin, hy_conv_w, hy_conv_b, hy_f1, hy_fb1, hy_f2, hy_fb2,
           hy_f3, hy_freq, hy_skip, hy_wo, hy_bo, router_w, router_b, moe_wg, moe_wu, moe_wd):
    bsz, l, d = x.shape
    depth = mod_w.shape[0]
    n_mixers = 4
    assert depth <= n_mixers
    assert bsz + 1 <= SUBLANES
    alpha = (2 * depth) ** 0.25
    c8 = jnp.zeros((SUBLANES, d), F32).at[:bsz].set(c).at[bsz].set(c_ctx)
    mod = _mod_call(c8, mod_w, mod_b)
    rwt = router_w.T.astype(BF16)
    rb = router_b.reshape(-1, 1)
    zeros = jnp.zeros((1, d), F32)
    ones = jnp.ones((1, d), F32)
    xl = x
    for i in range(depth):
        kind, j = i % n_mixers, i // n_mixers
        ml = mod[i, :bsz]
        sh1, sc1, g1, sh2, sc2, g2 = [ml[:, k * d:(k + 1) * d][:, None, :] for k in range(6)]
        bias, scale = zeros, ones
        if kind == 0:
            mc = mod[i, bsz]
            a = _attention_mixer(xl, ctx, sc1, sh1, mc[d:2 * d], mc[:d], attn_wqkv[j], attn_lam[j],
                                 attn_subln[j], i)
            w = attn_wo[j]
        elif kind == 1:
            a = _pool_call(xl, sc1, sh1)
            w = jax.scipy.linalg.block_diag(*[pool_w[j, g] for g in range(pool_w.shape[1])])
            scale = pool_scale[j].reshape(1, d)
        elif kind == 2:
            a = _fourier_mixer(xl, sc1, sh1)
            w, bias = fnet_w[j], fnet_b[j].reshape(1, d)
        else:
            a = _hyena_mixer(xl, sc1, sh1, hy_win[j], hy_bin[j], hy_conv_w[j], hy_conv_b[j], hy_f1[j],
                             hy_fb1[j], hy_f2[j], hy_fb2[j], hy_f3[j], hy_freq[j], hy_skip[j])
            w, bias = hy_wo[j], hy_bo[j].reshape(1, d)
        x1, tokx, ri, cnt = _post_call(a, w.astype(BF16), bias, scale, xl, g1, ln_g[i, 0:1], ln_b[i, 0:1],
                                       sc2, sh2, rwt, rb, alpha)
        xl = _moe(x1, tokx, ri, cnt, g2, ln_g[i, 1:2], ln_b[i, 1:2], moe_wg[i].astype(BF16),
                  moe_wu[i].astype(BF16), moe_wd[i].astype(BF16), alpha)
    return xl
```

```python
import functools
import math

import numpy as np
import jax
import jax.numpy as jnp
from jax import lax
from jax.experimental import pallas as pl
from jax.experimental.pallas import tpu as pltpu

F32 = jnp.float32
BF16 = jnp.bfloat16
I32 = jnp.int32

GRID_W = 64
ATT_HEADS = 8
ROPE_THETA = 10000.0
POOL_WINDOWS = (2, 4, 8, 16)
FFT_GROUPS = 4
HY_BANDS = 16
HY_TARGET = 1e-2
HY_FAST = 0.3
HY_SLOW = 1.5
N_GROUPS = 4
EXPERTS_PER_GROUP = 4
LN_EPS = 1e-5

LANES = 128
SUBLANES = 8
MXU_DIM = 256
VMEM_LIMIT = 56 * 1024 * 1024

ROW_TILE = 512
MOE_BLOCK = 256
N_PAIR_CLASSES = 24
CLASS_ROWS = 32
ATT_TQ = 256
ATT_TK = 512
HALO = 16


def _cparams(sem):
    return pltpu.CompilerParams(dimension_semantics=sem, vmem_limit_bytes=VMEM_LIMIT)


def _mm(a, b):
    return jnp.dot(a, b, preferred_element_type=F32)


def _mm_nt(a, b):
    return lax.dot_general(a, b, (((1,), (1,)), ((), ())), preferred_element_type=F32)


def _silu(x):
    return x * jax.nn.sigmoid(x)


def _layer_norm(u, g, b):
    mu = jnp.mean(u, axis=-1, keepdims=True)
    d = u - mu
    var = jnp.mean(d * d, axis=-1, keepdims=True)
    return d * lax.rsqrt(var + LN_EPS) * g + b


def _mod_kernel(c_ref, w_ref, b_ref, o_ref):
    s = _silu(c_ref[...]).astype(BF16)
    o_ref[0] = _mm(s, w_ref[0].astype(BF16)) + b_ref[0]


def _mod_call(c8, mod_w, mod_b):
    depth, d, n = mod_w.shape
    tn = 1536
    return pl.pallas_call(
        _mod_kernel,
        out_shape=jax.ShapeDtypeStruct((depth, 8, n), F32),
        grid=(depth, n // tn),
        in_specs=[
            pl.BlockSpec((8, d), lambda i, j: (0, 0)),
            pl.BlockSpec((1, d, tn), lambda i, j: (i, 0, j)),
            pl.BlockSpec((1, 1, tn), lambda i, j: (i, 0, j)),
        ],
        out_specs=pl.BlockSpec((1, 8, tn), lambda i, j: (i, 0, j)),
        compiler_params=_cparams(("arbitrary", "arbitrary")),
        name="mod",
    )(c8, mod_w, mod_b.reshape(depth, 1, n))


def _route_rows(logits_t):
    m = jnp.max(logits_t, axis=0, keepdims=True)
    e = jnp.exp(logits_t - m)
    p = e / jnp.sum(e, axis=0, keepdims=True)
    r = [p[i:i + 1, :] for i in range(N_GROUPS * EXPERTS_PER_GROUP)]
    gs = []
    for g in range(N_GROUPS):
        q = r[4 * g:4 * g + 4]
        best = q[0] + q[1]
        for (i, j) in ((0, 2), (0, 3), (1, 2), (1, 3), (2, 3)):
            best = jnp.maximum(best, q[i] + q[j])
        gs.append(best)
    gbest, gidx = gs[0], jnp.zeros_like(gs[0], dtype=I32)
    for g in range(1, N_GROUPS):
        upd = gs[g] > gbest
        gidx = jnp.where(upd, g, gidx)
        gbest = jnp.where(upd, gs[g], gbest)
    v = []
    for j in range(EXPERTS_PER_GROUP):
        vj = r[j]
        for g in range(1, N_GROUPS):
            vj = jnp.where(gidx == g, r[4 * g + j], vj)
        v.append(vj)

    def first_argmax(vals):
        best, idx = vals[0], jnp.zeros_like(gidx)
        for j in range(1, len(vals)):
            upd = vals[j] > best
            idx = jnp.where(upd, j, idx)
            best = jnp.where(upd, vals[j], best)
        return best, idx

    v1, i1 = first_argmax(v)
    v2, i2 = first_argmax([jnp.where(i1 == j, -1.0, v[j]) for j in range(EXPERTS_PER_GROUP)])
    tot = v1 + v2
    w1, w2 = v1 / tot, v2 / tot
    lo = jnp.minimum(i1, i2)
    hi = jnp.maximum(i1, i2)
    pair = jnp.where(lo == 0, hi - 1, jnp.where(lo == 1, hi + 1, 5))
    cls = gidx * 6 + pair
    first_is_lo = i1 < i2
    w_lo = jnp.where(first_is_lo, w1, w2)
    w_hi = jnp.where(first_is_lo, w2, w1)
    return cls, w_lo, w_hi


def _post_kernel(a_ref, w_ref, bias_ref, scale_ref, x_ref, g1_ref, lng_ref, lnb_ref, sc2_ref, sh2_ref,
                 rwt_ref, rb_ref, tri_ref, x1_ref, tokx_ref, ri_ref, cnt_ref, cnt_scr, *, alpha, d):
    first = jnp.logical_and(pl.program_id(0) == 0, pl.program_id(1) == 0)

    @pl.when(first)
    def _():
        cnt_scr[...] = jnp.zeros_like(cnt_scr)

    y = _mm(a_ref[...].astype(BF16), w_ref[...])
    y = (y + bias_ref[...]) * scale_ref[...]
    u = alpha * x_ref[...] + (1.0 + g1_ref[...]) * y
    x1 = _layer_norm(u, lng_ref[...], lnb_ref[...])
    x1_ref[...] = x1
    tok = x1 * (1.0 + sc2_ref[...]) + sh2_ref[...]
    tokx_ref[:, :d] = tok
    tm = tok.shape[0]
    logits_t = _mm_nt(rwt_ref[...], tok.astype(BF16)) + rb_ref[...]
    cls, w_lo, w_hi = _route_rows(logits_t)
    crow = lax.broadcasted_iota(I32, (CLASS_ROWS, tm), 0)
    onehot = (crow == cls).astype(F32)
    prefix = _mm(onehot.astype(BF16), tri_ref[...])
    base = cnt_scr[...]
    rank = jnp.sum(onehot * (prefix + base), axis=0, keepdims=True).astype(I32)
    cnt_new = base + jnp.sum(onehot, axis=1, keepdims=True)
    cnt_scr[...] = cnt_new
    cnt_ref[...] = cnt_new[:, :LANES]
    r8 = lax.broadcasted_iota(I32, (SUBLANES, tm), 0)
    ri_ref[...] = jnp.where(r8 == 0, cls, jnp.where(r8 == 1, rank, 0))
    r128 = lax.broadcasted_iota(I32, (LANES, tm), 0)
    wmat = jnp.where(r128 == 0, w_lo, jnp.where(r128 == 1, w_hi, 0.0))
    tokx_ref[:, d:] = wmat.T


def _post_call(a, w, bias, scale, x, g1, lng, lnb, sc2, sh2, rwt, rb, alpha):
    bsz, l, d = x.shape
    k = a.shape[-1]
    tm = min(ROW_TILE, l)
    nt = l // tm
    t = bsz * l
    tri = jnp.asarray(np.triu(np.ones((tm, tm), np.float32), 1), BF16)
    vec = lambda: pl.BlockSpec((1, d), lambda b, i: (0, 0))
    bvec = lambda: pl.BlockSpec((None, 1, d), lambda b, i: (b, 0, 0))
    return pl.pallas_call(
        functools.partial(_post_kernel, alpha=alpha, d=d),
        out_shape=(
            jax.ShapeDtypeStruct((bsz, l, d), F32),
            jax.ShapeDtypeStruct((t, d + LANES), F32),
            jax.ShapeDtypeStruct((SUBLANES, t), I32),
            jax.ShapeDtypeStruct((CLASS_ROWS, LANES), F32),
        ),
        grid=(bsz, nt),
        in_specs=[
            pl.BlockSpec((None, tm, k), lambda b, i: (b, i, 0)),
            pl.BlockSpec((k, d), lambda b, i: (0, 0)),
            vec(), vec(),
            pl.BlockSpec((None, tm, d), lambda b, i: (b, i, 0)),
            bvec(), vec(), vec(), bvec(), bvec(),
            pl.BlockSpec(rwt.shape, lambda b, i: (0, 0)),
            pl.BlockSpec(rb.shape, lambda b, i: (0, 0)),
            pl.BlockSpec((tm, tm), lambda b, i: (0, 0)),
        ],
        out_specs=(
            pl.BlockSpec((None, tm, d), lambda b, i: (b, i, 0)),
            pl.BlockSpec((tm, d + LANES), lambda b, i: (b * nt + i, 0)),
            pl.BlockSpec((SUBLANES, tm), lambda b, i: (0, b * nt + i)),
            pl.BlockSpec((CLASS_ROWS, LANES), lambda b, i: (0, 0)),
        ),
        scratch_shapes=[pltpu.VMEM((CLASS_ROWS, tm), F32)],
        compiler_params=_cparams(("arbitrary", "arbitrary")),
        name="post",
    )(a, w, bias, scale, x, g1, lng, lnb, sc2, sh2, rwt, rb, tri)


def _invperm_kernel(dest_ref, src_ref, *, t, p, chunk):
    phase = pl.program_id(0)
    base = pl.program_id(1) * chunk

    @pl.when(jnp.logical_and(phase == 0, base < p))
    def _():
        def fill(r, c):
            src_ref[base + r] = t + r
            return c
        lax.fori_loop(0, chunk, fill, 0, unroll=8)

    @pl.when(jnp.logical_and(phase == 1, base < t))
    def _():
        def put(i, c):
            src_ref[dest_ref[base + i]] = base + i
            return c
        lax.fori_loop(0, chunk, put, 0, unroll=8)


def _invperm_call(dest, n_rows):
    t = dest.shape[0]
    chunk = 2 * MOE_BLOCK
    assert n_rows % MOE_BLOCK == 0 and t % chunk == 0
    n_chunks = -(-n_rows // chunk)
    return pl.pallas_call(
        functools.partial(_invperm_kernel, t=t, p=n_rows, chunk=chunk),
        out_shape=jax.ShapeDtypeStruct((n_chunks * chunk,), I32),
        grid=(2, n_chunks),
        in_specs=[pl.BlockSpec(memory_space=pltpu.SMEM)],
        out_specs=pl.BlockSpec(memory_space=pltpu.SMEM),
        compiler_params=_cparams(("arbitrary", "arbitrary")),
        name="moe_invperm",
    )(dest)


def _expert_kernel(src_ref, ea_ref, eb_ref, nu_ref, tok_hbm, wga_ref, wua_ref, wda_ref, wgb_ref, wub_ref,
                   wdb_ref, y_hbm, xbuf, ybuf, gsem, ssem, *, d, fc, bm, t):
    del ea_ref, eb_ref
    b = pl.program_id(0)
    nu = nu_ref[0]
    slot = b % 2

    def gather_row(blk, s, r):
        tok = jnp.minimum(src_ref[blk * bm + r], t - 1)
        return pltpu.make_async_copy(tok_hbm.at[pl.ds(tok, 1)], xbuf.at[s, pl.ds(r, 1)], gsem.at[s])

    def scatter_row(blk, s, r):
        return pltpu.make_async_copy(ybuf.at[s, pl.ds(r, 1)], y_hbm.at[pl.ds(src_ref[blk * bm + r], 1)],
                                     ssem.at[s])

    def wait_gather(s):
        pltpu.make_async_copy(tok_hbm.at[pl.ds(0, bm)], xbuf.at[s], gsem.at[s]).wait()

    def wait_scatter(s):
        pltpu.make_async_copy(ybuf.at[s], y_hbm.at[pl.ds(0, bm)], ssem.at[s]).wait()

    @pl.when(b == 0)
    def _():
        ybuf[...] = jnp.zeros_like(ybuf)
        for s in range(2):
            cp = pltpu.make_async_copy(ybuf.at[s], y_hbm.at[pl.ds(t + s * bm, bm)], ssem.at[s])
            cp.start()
            cp.wait()
        for r in range(bm):
            gather_row(0, 0, r).start()

    @pl.when(b < nu)
    def _():
        wait_gather(slot)

        @pl.when(b >= 2)
        def _():
            wait_scatter(slot)

        nxt = jnp.minimum(b + 1, nu - 1)
        for r in range(bm):
            gather_row(nxt, 1 - slot, r).start()

        x = xbuf[slot, :, :d].astype(BF16)
        wts = xbuf[slot, :, d:]
        w_lo = wts[:, 0:1]
        w_hi = wts[:, 1:2]

        def ffn(wg_ref, wu_ref, wd_ref):
            f = wg_ref.shape[1]
            acc = None
            for c in range(f // fc):
                g = _mm(x, wg_ref[:, c * fc:(c + 1) * fc])
                u = _mm(x, wu_ref[:, c * fc:(c + 1) * fc])
                h = (_silu(g) * u).astype(BF16)
                part = _mm(h, wd_ref[c * fc:(c + 1) * fc, :])
                acc = part if acc is None else acc + part
            return acc

        ybuf[slot] = w_lo * ffn(wga_ref, wua_ref, wda_ref) + w_hi * ffn(wgb_ref, wub_ref, wdb_ref)
        for r in range(bm):
            scatter_row(b, slot, r).start()

        @pl.when(b == nu - 1)
        def _():
            wait_gather(1 - slot)
            wait_scatter(slot)

            @pl.when(b >= 1)
            def _():
                wait_scatter(1 - slot)


def _expert_call(src, blk_ea, blk_eb, n_used, tokx, wg, wu, wd):
    t, dx = tokx.shape
    d = dx - LANES
    f = wg.shape[-1]
    bm = MOE_BLOCK
    n_blocks = blk_ea.shape[0]
    amap = lambda b, s, ea, eb, nu: (ea[b], 0, 0)
    bmap = lambda b, s, ea, eb, nu: (eb[b], 0, 0)
    return pl.pallas_call(
        functools.partial(_expert_kernel, d=d, fc=512, bm=bm, t=t),
        out_shape=jax.ShapeDtypeStruct((t + 2 * bm, d), F32),
        grid_spec=pltpu.PrefetchScalarGridSpec(
            num_scalar_prefetch=4,
            grid=(n_blocks,),
            in_specs=[
                pl.BlockSpec(memory_space=pl.ANY),
                pl.BlockSpec((None, d, f), amap), pl.BlockSpec((None, d, f), amap), pl.BlockSpec((None, f, d), amap),
                pl.BlockSpec((None, d, f), bmap), pl.BlockSpec((None, d, f), bmap), pl.BlockSpec((None, f, d), bmap),
            ],
            out_specs=pl.BlockSpec(memory_space=pl.ANY),
            scratch_shapes=[pltpu.VMEM((2, bm, dx), F32), pltpu.VMEM((2, bm, d), F32),
                            pltpu.SemaphoreType.DMA((2,)), pltpu.SemaphoreType.DMA((2,))],
        ),
        compiler_params=_cparams(("arbitrary",)),
        name="experts",
    )(src, blk_ea, blk_eb, n_used, tokx, wg, wu, wd, wg, wu, wd)


def _combine_kernel(y2_ref, x1_ref, g2_ref, lng_ref, lnb_ref, o_ref, *, alpha):
    u = alpha * x1_ref[...] + (1.0 + g2_ref[...]) * y2_ref[...]
    o_ref[...] = _layer_norm(u, lng_ref[...], lnb_ref[...])


def _combine_call(y2, x1, g2, lng, lnb, alpha):
    bsz, l, d = x1.shape
    tm = min(ROW_TILE, l)
    nt = l // tm
    return pl.pallas_call(
        functools.partial(_combine_kernel, alpha=alpha),
        out_shape=jax.ShapeDtypeStruct((bsz, l, d), F32),
        grid=(bsz, nt),
        in_specs=[
            pl.BlockSpec((tm, d), lambda b, i: (b * nt + i, 0)),
            pl.BlockSpec((None, tm, d), lambda b, i: (b, i, 0)),
            pl.BlockSpec((None, 1, d), lambda b, i: (b, 0, 0)),
            pl.BlockSpec((1, d), lambda b, i: (0, 0)),
            pl.BlockSpec((1, d), lambda b, i: (0, 0)),
        ],
        out_specs=pl.BlockSpec((None, tm, d), lambda b, i: (b, i, 0)),
        compiler_params=_cparams(("arbitrary", "arbitrary")),
        name="combine",
    )(y2, x1, g2, lng, lnb)


def _pair_tables():
    ta, tb = [], []
    for g in range(N_GROUPS):
        for a in range(EXPERTS_PER_GROUP):
            for b in range(a + 1, EXPERTS_PER_GROUP):
                ta.append(g * EXPERTS_PER_GROUP + a)
                tb.append(g * EXPERTS_PER_GROUP + b)
    return np.asarray(ta, np.int32), np.asarray(tb, np.int32)


def _moe(x1, tokx, ri, cnt, g2, lng, lnb, wg, wu, wd, alpha):
    bsz, l, d = x1.shape
    t = bsz * l
    bm = MOE_BLOCK
    n_blocks = t // bm + N_PAIR_CLASSES
    counts = cnt[:N_PAIR_CLASSES, 0].astype(I32)
    nblk = (counts + bm - 1) // bm
    cend = jnp.cumsum(nblk)
    cstart = cend - nblk
    dest = cstart[ri[0]] * bm + ri[1]
    blk = jnp.arange(n_blocks, dtype=I32)
    blk_cls = jnp.minimum(jnp.sum((cend[None, :] <= blk[:, None]).astype(I32), axis=1), N_PAIR_CLASSES - 1)
    ta, tb = _pair_tables()
    blk_ea = jnp.asarray(ta)[blk_cls]
    blk_eb = jnp.asarray(tb)[blk_cls]
    n_used = cend[-1:].astype(I32)
    src = _invperm_call(dest, n_blocks * bm)
    y2 = _expert_call(src, blk_ea, blk_eb, n_used, tokx, wg, wu, wd)
    return _combine_call(y2, x1, g2, lng, lnb, alpha)


def _qkv_kernel(x_ref, sc_ref, sh_ref, w_ref, *rest, rope, groups, d, q_scale):
    if rope:
        tab_ref, out_refs = rest[0], rest[1:]
    else:
        tab_ref, out_refs = None, rest
    h = (x_ref[...] * (1.0 + sc_ref[...]) + sh_ref[...]).astype(BF16)
    for (g, transposed), o_ref in zip(groups, out_refs):
        for c in range(d // MXU_DIM):
            col = g * d + c * MXU_DIM
            z = _mm(h, w_ref[:, col:col + MXU_DIM])
            if rope and g < 2:
                heads = []
                for hh in range(MXU_DIM // LANES):
                    zh = z[:, hh * LANES:(hh + 1) * LANES]
                    heads.append(zh * tab_ref[0] + pltpu.roll(zh, LANES - 16, 1) * tab_ref[1]
                                 + pltpu.roll(zh, 16, 1) * tab_ref[2])
                z = jnp.concatenate(heads, axis=1)
            if g == 0:
                z = z * q_scale
            if transposed:
                o_ref[c * MXU_DIM:(c + 1) * MXU_DIM, :] = z.T.astype(BF16)
            else:
                o_ref[:, c * MXU_DIM:(c + 1) * MXU_DIM] = z.astype(BF16)


def _qkv_call(x, sc, sh, w, tab, groups, q_scale):
    bsz, l, d = x.shape
    tm = min(ROW_TILE, l)
    rope = tab is not None
    in_specs = [
        pl.BlockSpec((None, tm, d), lambda b, i: (b, i, 0)),
        pl.BlockSpec((None, 1, d), lambda b, i: (b, 0, 0)),
        pl.BlockSpec((None, 1, d), lambda b, i: (b, 0, 0)),
        pl.BlockSpec(w.shape, lambda b, i: (0, 0)),
    ]
    args = [x, sc, sh, w]
    if rope:
        in_specs.append(pl.BlockSpec((3, tm, LANES), lambda b, i: (0, i, 0)))
        args.append(tab)
    out_shape, out_specs = [], []
    for (_, transposed) in groups:
        if transposed:
            out_shape.append(jax.ShapeDtypeStruct((bsz, d, l), BF16))
            out_specs.append(pl.BlockSpec((None, d, tm), lambda b, i: (b, 0, i)))
        else:
            out_shape.append(jax.ShapeDtypeStruct((bsz, l, d), BF16))
            out_specs.append(pl.BlockSpec((None, tm, d), lambda b, i: (b, i, 0)))
    return pl.pallas_call(
        functools.partial(_qkv_kernel, rope=rope, groups=groups, d=d, q_scale=q_scale),
        out_shape=tuple(out_shape),
        grid=(bsz, l // tm),
        in_specs=in_specs,
        out_specs=tuple(out_specs),
        compiler_params=_cparams(("arbitrary", "arbitrary")),
        name="qkv",
    )(*args)


def _attn_kernel(q_ref, kt_ref, v_ref, kct_ref, vc_ref, lam_ref, subln_ref, o_ref, m_scr, acc_scr, s_scr, p_scr,
                 a_scr, *, tq, tk, lam_init):
    q = q_ref[...]
    hd = q.shape[1]
    lane = lax.broadcasted_iota(I32, q.shape, 1)
    zero = jnp.zeros_like(q)
    lhs = jnp.concatenate([jnp.where(lane < hd // 2, q, zero), jnp.where(lane >= hd // 2, q, zero)], axis=0)
    m_scr[...] = jnp.full_like(m_scr, -jnp.inf)
    acc_scr[...] = jnp.zeros_like(acc_scr)

    def softmax(s):
        m_old = m_scr[...]
        m_new = jnp.maximum(m_old, jnp.max(s, axis=1, keepdims=True))
        m_scr[...] = m_new
        p = jnp.exp2(s - jnp.concatenate([m_new] * (s.shape[1] // LANES), axis=1)).astype(BF16)
        return p, jnp.exp2(m_old - m_new)

    def accumulate(p, a, vb):
        tkk = vb.shape[0]
        ones_col = jnp.where(lax.broadcasted_iota(I32, (tkk, LANES), 1) == 0, 1.0, 0.0).astype(BF16)
        vext = jnp.concatenate([vb, ones_col], axis=1)
        acc_scr[...] = jnp.concatenate([a, a], axis=1) * acc_scr[...] + _mm(p, vext)

    def k_chunk(j):
        return kt_ref[:, pl.ds(pl.multiple_of(j * tk, tk), tk)]

    def v_chunk(j):
        return v_ref[pl.ds(pl.multiple_of(j * tk, tk), tk), :]

    n_chunks = v_ref.shape[0] // tk
    s_scr[0] = _mm(lhs, k_chunk(0))
    if n_chunks > 1:
        s_scr[1] = _mm(lhs, k_chunk(1))
    p0, a0 = softmax(s_scr[0])
    p_scr[0] = p0
    a_scr[0] = a0

    def stage(j, slot):
        s_next = _mm(lhs, k_chunk(j + 1))
        accumulate(p_scr[1 - slot], a_scr[1 - slot], v_chunk(j - 1))
        p, a = softmax(s_scr[slot])
        s_scr[1 - slot] = s_next
        p_scr[slot] = p
        a_scr[slot] = a

    def pair(i, c):
        stage(2 * i + 1, 1)
        stage(2 * i + 2, 0)
        return c

    n_stages = max(n_chunks - 2, 0)
    lax.fori_loop(0, n_stages // 2, pair, 0)
    if n_stages % 2:
        stage(n_chunks - 2, (n_chunks - 2) % 2)
    last = (n_chunks - 1) % 2
    sc = _mm(lhs, kct_ref[...])
    if n_chunks > 1:
        accumulate(p_scr[1 - last], a_scr[1 - last], v_chunk(n_chunks - 2))
        p, a = softmax(s_scr[last])
    else:
        p, a = p0, a0
    accumulate(p, a, v_chunk(n_chunks - 1))
    pc, ac = softmax(sc)
    accumulate(pc, ac, vc_ref[...])

    lam = lam_ref[...]
    lam_full = (jnp.exp(jnp.sum(lam[0:1] * lam[1:2], axis=1, keepdims=True))
                - jnp.exp(jnp.sum(lam[2:3] * lam[3:4], axis=1, keepdims=True)) + lam_init)
    acc = acc_scr[...]
    o12 = acc[:, :hd] / acc[:, hd:hd + 1]
    o = o12[:tq] - lam_full * o12[tq:]
    o = o * lax.rsqrt(jnp.mean(o * o, axis=1, keepdims=True) + LN_EPS)
    o = o * (subln_ref[...] * (1.0 - lam_init))
    o_ref[...] = o.astype(BF16)


def _attn_call(q, kt, v, kct, vc, lam, subln_row, lam_init):
    bsz, l, d = q.shape
    lc = vc.shape[1]
    hd = d // ATT_HEADS
    tq = min(ATT_TQ, l)
    tk = min(ATT_TK, l)
    return pl.pallas_call(
        functools.partial(_attn_kernel, tq=tq, tk=tk, lam_init=lam_init),
        out_shape=jax.ShapeDtypeStruct((bsz, l, d), BF16),
        grid=(bsz, ATT_HEADS, l // tq),
        in_specs=[
            pl.BlockSpec((None, tq, hd), lambda b, h, i: (b, i, h)),
            pl.BlockSpec((None, hd, l), lambda b, h, i: (b, h, 0)),
            pl.BlockSpec((None, l, hd), lambda b, h, i: (b, 0, h)),
            pl.BlockSpec((None, hd, lc), lambda b, h, i: (b, h, 0)),
            pl.BlockSpec((None, lc, hd), lambda b, h, i: (b, 0, h)),
            pl.BlockSpec(lam.shape, lambda b, h, i: (0, 0)),
            pl.BlockSpec(subln_row.shape, lambda b, h, i: (0, 0)),
        ],
        out_specs=pl.BlockSpec((None, tq, hd), lambda b, h, i: (b, i, h)),
        scratch_shapes=[pltpu.VMEM((2 * tq, LANES), F32), pltpu.VMEM((2 * tq, hd + LANES), F32),
                        pltpu.VMEM((2, 2 * tq, tk), F32), pltpu.VMEM((2, 2 * tq, tk), BF16),
                        pltpu.VMEM((2, 2 * tq, LANES), F32)],
        compiler_params=_cparams(("arbitrary", "arbitrary", "arbitrary")),
        name="diff_attn",
    )(q, kt, v, kct, vc, lam, subln_row)


def _rope_tables(l, hd):
    dh = hd // 2
    quarter = dh // 4
    pos = np.arange(l)
    row = (pos // GRID_W).astype(np.float32)
    col = (pos % GRID_W).astype(np.float32)
    inv = (ROPE_THETA ** (-np.arange(0, dh // 2, 2, dtype=np.float32) / (dh // 2))).astype(np.float32)
    ar = row[:, None] * inv
    ac = col[:, None] * inv
    ang = np.concatenate([ar, ar, ac, ac], axis=-1).astype(np.float32)
    cos, sin = np.cos(ang), np.sin(ang)
    lane = np.arange(dh)
    first = (lane % (2 * quarter)) < quarter
    sin_a = np.where(first, -sin, 0.0)
    sin_b = np.where(first, 0.0, sin)
    tab = np.stack([np.tile(cos, (1, 2)), np.tile(sin_a, (1, 2)), np.tile(sin_b, (1, 2))]).astype(np.float32)
    return jnp.asarray(tab)


def _attention_mixer(x, ctx, sc1, sh1, csc1, csh1, wqkv, lam, subln, layer_idx):
    bsz, l, d = x.shape
    hd = d // ATT_HEADS
    lam_init = 0.8 - 0.6 * math.exp(-0.3 * layer_idx)
    w = wqkv.astype(BF16)
    q, kt, v = _qkv_call(x, sc1, sh1, w, _rope_tables(l, hd), ((0, False), (1, True), (2, False)),
                         (hd // 2) ** -0.5 * math.log2(math.e))
    ones = jnp.ones((bsz, 1, 1), F32)
    kct, vc = _qkv_call(ctx, csc1[None, None, :] * ones, csh1[None, None, :] * ones, w, None,
                        ((1, True), (2, False)), 1.0)
    return _attn_call(q, kt, v, kct, vc, lam, subln.reshape(1, hd), lam_init)


def _halo_specs(tm, d, l):
    per = tm // HALO
    last = l // HALO - 1
    return [
        pl.BlockSpec((None, tm, d), lambda b, i: (b, i, 0)),
        pl.BlockSpec((None, HALO, d), lambda b, i: (b, jnp.maximum(i * per - 1, 0), 0)),
        pl.BlockSpec((None, HALO, d), lambda b, i: (b, jnp.minimum((i + 1) * per, last), 0)),
    ]


def _pool_kernel(x_ref, xp_ref, xn_ref, sc_ref, sh_ref, o_ref, ext, *, tm, l):
    i = pl.program_id(1)
    nt = pl.num_programs(1)
    sc = 1.0 + sc_ref[...]
    sh = sh_ref[...]
    ext[HALO:HALO + tm, :] = x_ref[...] * sc + sh
    ext[0:HALO, :] = jnp.where(i > 0, xp_ref[...] * sc + sh, 0.0)
    ext[HALO + tm:, :] = jnp.where(i < nt - 1, xn_ref[...] * sc + sh, 0.0)
    gw = x_ref.shape[1] // len(POOL_WINDOWS)
    t = i * tm + lax.broadcasted_iota(I32, (tm, gw), 0)
    for g, win in enumerate(POOL_WINDOWS):
        c0 = g * gw
        acc = None
        for j in range(-(win // 2), win - win // 2):
            part = ext[HALO + j:HALO + j + tm, c0:c0 + gw]
            acc = part if acc is None else acc + part
        lo = jnp.maximum(t - win // 2, 0)
        hi = jnp.minimum(t - win // 2 + win, l)
        mean = acc / (hi - lo).astype(F32)
        o_ref[:, c0:c0 + gw] = (mean - ext[HALO:HALO + tm, c0:c0 + gw]).astype(BF16)


def _pool_call(x, sc1, sh1):
    bsz, l, d = x.shape
    tm = min(ROW_TILE, l)
    bvec = lambda: pl.BlockSpec((None, 1, d), lambda b, i: (b, 0, 0))
    return pl.pallas_call(
        functools.partial(_pool_kernel, tm=tm, l=l),
        out_shape=jax.ShapeDtypeStruct((bsz, l, d), BF16),
        grid=(bsz, l // tm),
        in_specs=_halo_specs(tm, d, l) + [bvec(), bvec()],
        out_specs=pl.BlockSpec((None, tm, d), lambda b, i: (b, i, 0)),
        scratch_shapes=[pltpu.VMEM((tm + 2 * HALO, d), F32)],
        compiler_params=_cparams(("arbitrary", "arbitrary")),
        name="pool",
    )(x, x, x, sc1, sh1)


KRON = SUBLANES
INNER = 128
N_RB = INNER // KRON


def _kron_rows(m):
    p, q = m.shape
    return np.einsum("pq,jk->pjqk", m, np.eye(KRON)).reshape(p * KRON, q * KRON)


def _outer_kernel(*refs, n_in):
    m_ref, x_refs, o_ref = refs[0], refs[1:1 + n_in], refs[1 + n_in]
    acc = None
    col = 0
    for x_ref in x_refs:
        a, j, d = x_ref.shape
        x = x_ref[...].reshape(a * j, d).astype(BF16)
        part = _mm(m_ref[:, col:col + a * j], x)
        acc = part if acc is None else acc + part
        col += a * j
    o_ref[...] = acc.astype(o_ref.dtype)


def _outer_fwd_call(m, xs, a_out):
    bsz, rows, d = xs[0].shape
    a_in = rows // INNER
    xs4 = [x.reshape(bsz, a_in, N_RB, KRON, d) for x in xs]
    return pl.pallas_call(
        functools.partial(_outer_kernel, n_in=len(xs)),
        out_shape=jax.ShapeDtypeStruct((bsz, N_RB, a_out * 2 * KRON, d), BF16),
        grid=(bsz, N_RB),
        in_specs=[pl.BlockSpec(m.shape, lambda b, r: (0, 0))]
        + [pl.BlockSpec((None, a_in, None, KRON, d), lambda b, r: (b, 0, r, 0, 0)) for _ in xs],
        out_specs=pl.BlockSpec((None, None, a_out * 2 * KRON, d), lambda b, r: (b, r, 0, 0)),
        compiler_params=_cparams(("arbitrary", "arbitrary")),
        name="dft_outer",
    )(m, *xs4)


def _fn_chan_kernel(x_ref, sc_ref, sh_ref, cs_ref, pr_ref, pi_ref, *, gw):
    h = (x_ref[...] * (1.0 + sc_ref[...]) + sh_ref[...]).astype(BF16)
    for g in range(h.shape[1] // gw):
        z = _mm(h[:, g * gw:(g + 1) * gw], cs_ref[...])
        pr_ref[:, g * gw:(g + 1) * gw] = z[:, :gw]
        pi_ref[:, g * gw:(g + 1) * gw] = z[:, gw:]


def _fn_chan_call(x, sc1, sh1):
    bsz, l, d = x.shape
    gw = d // FFT_GROUPS
    tm = min(ROW_TILE, l)
    ang = 2.0 * np.pi * np.outer(np.arange(gw), np.arange(gw)) / gw
    cs = jnp.asarray(np.concatenate([np.cos(ang), -np.sin(ang)], axis=1), BF16)
    bvec = lambda: pl.BlockSpec((None, 1, d), lambda b, i: (b, 0, 0))
    tile = lambda: pl.BlockSpec((None, tm, d), lambda b, i: (b, i, 0))
    return pl.pallas_call(
        functools.partial(_fn_chan_kernel, gw=gw),
        out_shape=(jax.ShapeDtypeStruct((bsz, l, d), F32), jax.ShapeDtypeStruct((bsz, l, d), F32)),
        grid=(bsz, l // tm),
        in_specs=[tile(), bvec(), bvec(), pl.BlockSpec(cs.shape, lambda b, i: (0, 0))],
        out_specs=(tile(), tile()),
        compiler_params=_cparams(("arbitrary", "arbitrary")),
        name="fnet_channel_dft",
    )(x, sc1, sh1, cs)


def _fn_inner_kernel(m_ref, x_ref, o_ref):
    rb, j1, cj, d = x_ref.shape
    x = x_ref[...].reshape(rb * j1 * cj, d)
    y = _mm(m_ref[...], x)
    o_ref[...] = y.reshape(o_ref.shape)


def _fn_inner_call(m2, spec, a):
    bsz, _, _, d = spec.shape
    spec5 = spec.reshape(bsz, N_RB, a, 2 * KRON, d)
    out = pl.pallas_call(
        _fn_inner_kernel,
        out_shape=jax.ShapeDtypeStruct((bsz, INNER, a, d), F32),
        grid=(a // KRON, bsz),
        in_specs=[
            pl.BlockSpec((None,) + m2.shape[1:], lambda s, b: (s, 0, 0)),
            pl.BlockSpec((None, N_RB, KRON, 2 * KRON, d), lambda s, b: (b, 0, s, 0, 0)),
        ],
        out_specs=pl.BlockSpec((None, INNER, KRON, d), lambda s, b: (b, 0, s, 0)),
        compiler_params=_cparams(("arbitrary", "arbitrary")),
        name="fnet_inner_dft",
    )(m2, spec5)
    return out.reshape(bsz, INNER * a, d)


def _fnet_tables(l, gw):
    a = l // INNER
    k1 = np.arange(a)
    ang1 = 2.0 * np.pi * np.outer(k1, k1) / a
    c1, s1 = np.cos(ang1), np.sin(ang1)
    m1 = np.zeros((a, 2, 2, a))
    m1[:, 0, 0, :], m1[:, 0, 1, :] = c1, s1
    m1[:, 1, 0, :], m1[:, 1, 1, :] = -s1, c1
    m1k = np.einsum("pcea,jk->pcjeak", m1, np.eye(KRON)).reshape(a * 2 * KRON, 2 * a * KRON)
    rb, j1c, cc, jj = np.meshgrid(np.arange(N_RB), np.arange(KRON), np.arange(2), np.arange(KRON), indexing="ij")
    l2 = (rb * KRON + jj).reshape(-1)
    j1c, cc = j1c.reshape(-1), cc.reshape(-1)
    norm = 1.0 / math.sqrt(l * gw)
    alpha = 2.0 * np.pi * np.outer(np.arange(INNER), l2) / INNER
    p_tab = np.where(cc == 0, np.cos(alpha), np.sin(alpha)) * norm
    q_tab = np.where(cc == 0, -np.sin(alpha), np.cos(alpha)) * norm
    k1_col = KRON * np.arange(a // KRON)[:, None] + j1c[None, :]
    beta = 2.0 * np.pi * k1_col * l2[None, :] / l
    f = lambda t: jnp.asarray(t, F32)
    t3 = (f(p_tab)[None] * f(np.cos(beta))[:, None, :] + f(q_tab)[None] * f(np.sin(beta))[:, None, :])
    mask = f(j1c[None, :] == np.arange(KRON)[:, None])
    m2 = (t3[:, :, None, :] * mask[None, None]).reshape(a // KRON, INNER * KRON, l2.shape[0])
    return jnp.asarray(m1k, BF16), m2.astype(BF16)


def _fourier_mixer(x, sc1, sh1):
    bsz, l, d = x.shape
    a = l // INNER
    m1k, m2 = _fnet_tables(l, d // FFT_GROUPS)
    pr, pi = _fn_chan_call(x, sc1, sh1)
    spec = _outer_fwd_call(m1k, [pr, pi], a)
    return _fn_inner_call(m2, spec, a)


def _hy_front_kernel(x_ref, xp_ref, xn_ref, sc_ref, sh_ref, w_ref, bin_ref, cw_ref, cb_ref,
                     vg_ref, x0_ref, hext, zscr, *, tm, l, d):
    i = pl.program_id(1)
    sc = 1.0 + sc_ref[...]
    sh = sh_ref[...]
    hext[HALO:HALO + tm, :] = (x_ref[...] * sc + sh).astype(BF16)
    hext[0:HALO, :] = (xp_ref[...] * sc + sh).astype(BF16)
    hext[HALO + tm:, :] = (xn_ref[...] * sc + sh).astype(BF16)
    pos = i * tm - HALO + lax.broadcasted_iota(I32, (tm + 2 * HALO, MXU_DIM), 0)

    def conv(col):
        z = _mm(hext[...], w_ref[:, col:col + MXU_DIM]) + bin_ref[:, col:col + MXU_DIM]
        zscr[...] = jnp.where(pos >= 0, jnp.where(pos < l, z, 0.0), 0.0)
        cw = cw_ref[:, col:col + MXU_DIM]
        return (zscr[HALO - 1:HALO - 1 + tm, :] * cw[0:1] + zscr[HALO:HALO + tm, :] * cw[1:2]
                + zscr[HALO + 1:HALO + 1 + tm, :] * cw[2:3] + cb_ref[:, col:col + MXU_DIM])

    for c in range(d // MXU_DIM):
        cols = slice(c * MXU_DIM, (c + 1) * MXU_DIM)
        x0_ref[:, cols] = conv(c * MXU_DIM)
        x1 = conv(d + c * MXU_DIM)
        v = conv(2 * d + c * MXU_DIM)
        vg_ref[:, cols] = v * x1


def _hy_front_call(x, sc1, sh1, win, bin_, conv_w, conv_b):
    bsz, l, d = x.shape
    tm = min(ROW_TILE, l)
    n3 = win.shape[1]
    bvec = lambda: pl.BlockSpec((None, 1, d), lambda b, i: (b, 0, 0))
    full = lambda a: pl.BlockSpec(a.shape, lambda b, i: (0, 0))
    tile = lambda: pl.BlockSpec((None, tm, d), lambda b, i: (b, i, 0))
    args = (win.astype(BF16), bin_.reshape(1, n3), conv_w, conv_b.reshape(1, n3))
    return pl.pallas_call(
        functools.partial(_hy_front_kernel, tm=tm, l=l, d=d),
        out_shape=(jax.ShapeDtypeStruct((bsz, l, d), F32), jax.ShapeDtypeStruct((bsz, l, d), F32)),
        grid=(bsz, l // tm),
        in_specs=_halo_specs(tm, d, l) + [bvec(), bvec()] + [full(a) for a in args],
        out_specs=(tile(), tile()),
        scratch_shapes=[pltpu.VMEM((tm + 2 * HALO, d), BF16), pltpu.VMEM((tm + 2 * HALO, MXU_DIM), F32)],
        compiler_params=_cparams(("arbitrary", "arbitrary")),
        name="hyena_front",
    )(x, x, x, sc1, sh1, *args)


def _hy_filter_kernel(z_ref, f1_ref, fb1_ref, f2_ref, fb2_ref, f3_ref, fr_ref, del_ref, o_ref, *, tf, l):
    z = z_ref[...]
    fr = fr_ref[...]
    a = jnp.sin(fr * (_mm(z, f1_ref[...]) + fb1_ref[...]))
    a = jnp.sin(fr * (_mm(a, f2_ref[...]) + fb2_ref[...]))
    k = _mm(a, f3_ref[...])
    window = jnp.exp(-z[:, 0:1] * del_ref[...])
    row = pl.program_id(0) * tf + lax.broadcasted_iota(I32, k.shape, 0)
    o_ref[...] = jnp.where(row == l, 0.0, k * window)


def _hy_filter_call(l, d, f1, fb1, f2, fb2, f3, freq):
    n = 2 * l
    tf = min(ROW_TILE, l)
    hid = f1.shape[1]
    pos = jnp.concatenate([jnp.arange(l, dtype=F32), jnp.zeros((1,), F32),
                           jnp.arange(l - 1, 0, -1, dtype=F32)])
    bands = jnp.linspace(1e-4, HY_BANDS - 1, HY_BANDS, dtype=F32)
    ang = (2.0 * math.pi / l) * pos[:, None] * bands[None, :]
    feat = jnp.concatenate([(pos / l)[:, None], jnp.cos(ang), jnp.sin(ang)], axis=-1)
    kin = 64
    feat = jnp.pad(feat, ((0, 0), (0, kin - feat.shape[1])))
    f1p = jnp.pad(f1, ((0, kin - f1.shape[0]), (0, 0)))
    deltas = np.abs(np.linspace(math.log(HY_TARGET) / HY_SLOW, math.log(HY_TARGET) / HY_FAST, d,
                                dtype=np.float32)).reshape(1, d)
    half = l // tf
    full = lambda a: pl.BlockSpec(a.shape, lambda i: (0, 0))
    args = (f1p, fb1.reshape(1, hid), f2, fb2.reshape(1, hid))
    tail = (freq.reshape(1, hid), jnp.asarray(deltas))
    return pl.pallas_call(
        functools.partial(_hy_filter_kernel, tf=tf, l=l),
        out_shape=jax.ShapeDtypeStruct((n, d), F32),
        grid=(n // tf,),
        in_specs=[pl.BlockSpec((tf, kin), lambda i: (i, 0))] + [full(a) for a in args]
        + [pl.BlockSpec((hid, d), lambda i: (0, i // half))] + [full(a) for a in tail],
        out_specs=pl.BlockSpec((tf, d), lambda i: (i, 0)),
        compiler_params=_cparams(("arbitrary",)),
        name="hyena_filter",
    )(feat, *args, f3, *tail)


def _hy_kspec_kernel(mf_ref, x_ref, o_ref):
    rb, cj, d = x_ref.shape
    o_ref[...] = _mm(mf_ref[...], x_ref[...].reshape(rb * cj, d))


def _hy_kspec_call(mf2, spec, a):
    _, _, _, d = spec.shape
    spec5 = spec.reshape(1, N_RB, a, 2 * KRON, d)
    return pl.pallas_call(
        _hy_kspec_kernel,
        out_shape=jax.ShapeDtypeStruct((a, 2 * INNER, d), F32),
        grid=(a,),
        in_specs=[
            pl.BlockSpec((None, 2 * INNER, 2 * INNER), lambda k: (k, 0, 0)),
            pl.BlockSpec((None, N_RB, None, 2 * KRON, d), lambda k: (0, 0, k, 0, 0)),
        ],
        out_specs=pl.BlockSpec((None, 2 * INNER, d), lambda k: (k, 0, 0)),
        compiler_params=_cparams(("arbitrary",)),
        name="hyena_filter_spectrum",
    )(mf2, spec5)


def _hy_mid_kernel(mf_ref, mi_ref, k_ref, x_ref, o_ref):
    rb, cj, d = x_ref.shape
    v = _mm(mf_ref[...], x_ref[...].reshape(rb * cj, d))
    kk = k_ref[...]
    vr, vi = v[:INNER], v[INNER:]
    kr, ki = kk[:INNER], kk[INNER:]
    z = jnp.concatenate([vr * kr - vi * ki, vr * ki + vi * kr], axis=0).astype(BF16)
    y = _mm(mi_ref[...], z)
    o_ref[...] = y.astype(BF16).reshape(rb, cj, d)


def _hy_mid_call(mf2, mi1, kspec, spec, a):
    bsz, _, _, d = spec.shape
    spec5 = spec.reshape(bsz, N_RB, a, 2 * KRON, d)
    mat = lambda: pl.BlockSpec((None, 2 * INNER, 2 * INNER), lambda k, b: (k, 0, 0))
    blk = lambda: pl.BlockSpec((None, N_RB, None, 2 * KRON, d), lambda k, b: (b, 0, k, 0, 0))
    out = pl.pallas_call(
        _hy_mid_kernel,
        out_shape=jax.ShapeDtypeStruct(spec5.shape, BF16),
        grid=(a, bsz),
        in_specs=[mat(), mat(), pl.BlockSpec((None, 2 * INNER, d), lambda k, b: (k, 0, 0)), blk()],
        out_specs=blk(),
        compiler_params=_cparams(("arbitrary", "arbitrary")),
        name="hyena_spectral_product",
    )(mf2, mi1, kspec, spec5)
    return out.reshape(spec.shape)


def _hy_out_kernel(m_ref, b_ref, vg_ref, x0_ref, skip_ref, o_ref):
    a2, j, d = vg_ref.shape
    y = _mm(m_ref[...], b_ref[...])
    vg = vg_ref[...].reshape(a2 * j, d)
    x0 = x0_ref[...].reshape(a2 * j, d)
    o_ref[...] = ((y + vg * skip_ref[...]) * x0).reshape(a2, j, d)


def _hy_out_call(mi2, spec, vg, x0, skip):
    bsz, l, d = vg.shape
    a2 = l // INNER
    view = lambda t: t.reshape(bsz, a2, N_RB, KRON, d)
    blk = lambda: pl.BlockSpec((None, a2, None, KRON, d), lambda b, r: (b, 0, r, 0, 0))
    out = pl.pallas_call(
        _hy_out_kernel,
        out_shape=jax.ShapeDtypeStruct((bsz, a2, N_RB, KRON, d), F32),
        grid=(bsz, N_RB),
        in_specs=[
            pl.BlockSpec(mi2.shape, lambda b, r: (0, 0)),
            pl.BlockSpec((None, None) + spec.shape[2:], lambda b, r: (b, r, 0, 0)),
            blk(), blk(),
            pl.BlockSpec((1, d), lambda b, r: (0, 0)),
        ],
        out_specs=blk(),
        compiler_params=_cparams(("arbitrary", "arbitrary")),
        name="hyena_inverse_outer",
    )(mi2, spec, view(vg), view(x0), skip.reshape(1, d))
    return out.reshape(bsz, l, d)


def _hyena_tables(l):
    n = 2 * l
    a = n // INNER
    kb = np.arange(a)
    ang = 2.0 * np.pi * np.outer(kb, kb) / a
    fwd = np.stack([np.cos(ang), -np.sin(ang)], axis=1)
    mf1 = _kron_rows(fwd[:, :, :a // 2].reshape(2 * a, a // 2))
    mk1 = _kron_rows(fwd.reshape(2 * a, a))
    inv = np.stack([np.cos(ang), -np.sin(ang)], axis=2)[:a // 2] / n
    mi2 = _kron_rows(inv.reshape(a // 2, 2 * a))
    rb, cc, jj = np.meshgrid(np.arange(N_RB), np.arange(2), np.arange(KRON), indexing="ij")
    li = (rb * KRON + jj).reshape(-1)
    cc = cc.reshape(-1)
    alpha = 2.0 * np.pi * np.outer(np.arange(INNER), li) / INNER
    ca, sa = np.cos(alpha), np.sin(alpha)
    p_tab = np.concatenate([np.where(cc == 0, ca, sa), np.where(cc == 0, -sa, ca)], axis=0)
    q_tab = np.concatenate([np.where(cc == 0, -sa, ca), np.where(cc == 0, -ca, -sa)], axis=0)
    beta = 2.0 * np.pi * np.outer(kb, li) / n
    f = lambda t: jnp.asarray(t, F32)
    cb, sb = f(np.cos(beta)), f(np.sin(beta))
    mf2 = (f(p_tab)[None] * cb[:, None, :] + f(q_tab)[None] * sb[:, None, :]).astype(BF16)
    mi1 = (f(p_tab.T)[None] * cb[:, :, None] + f(q_tab.T)[None] * sb[:, :, None]).astype(BF16)
    return (jnp.asarray(mf1, BF16), jnp.asarray(mk1, BF16), jnp.asarray(mi2, BF16), mf2, mi1)


def _hyena_mixer(x, sc1, sh1, win, bin_, conv_w, conv_b, f1, fb1, f2, fb2, f3, freq, skip):
    bsz, l, d = x.shape
    a = 2 * l // INNER
    mf1, mk1, mi2, mf2, mi1 = _hyena_tables(l)
    vg, x0 = _hy_front_call(x, sc1, sh1, win, bin_, conv_w, conv_b)
    kfull = _hy_filter_call(l, d, f1, fb1, f2, fb2, f3, freq)
    kspec = _hy_kspec_call(mf2, _outer_fwd_call(mk1, [kfull[None]], a), a)
    spec = _outer_fwd_call(mf1, [vg], a)
    spec = _hy_mid_call(mf2, mi1, kspec, spec, a)
    return _hy_out_call(mi2, spec, vg, x0, skip)


def kernel(x, c, ctx, c_ctx, mod_w, mod_b, ln_g, ln_b, attn_wqkv, attn_wo, attn_lam, attn_subln, pool_w,
           pool_scale, fnet_w, fnet_b, hy_win, hy_bin, hy_conv_w, hy_conv_b, hy_f1, hy_fb1, hy_f2, hy_fb2,
           hy_f3, hy_freq, hy_skip, hy_wo, hy_bo, router_w, router_b, moe_wg, moe_wu, moe_wd):
    bsz, l, d = x.shape
    depth = mod_w.shape[0]
    n_mixers = 4
    assert depth <= n_mixers
    assert bsz + 1 <= SUBLANES
    alpha = (2 * depth) ** 0.25
    c8 = jnp.zeros((SUBLANES, d), F32).at[:bsz].set(c).at[bsz].set(c_ctx)
    mod = _mod_call(c8, mod_w, mod_b)
    rwt = router_w.T.astype(BF16)
    rb = router_b.reshape(-1, 1)
    zeros = jnp.zeros((1, d), F32)
    ones = jnp.ones((1, d), F32)
    xl = x
    for i in range(depth):
        kind, j = i % n_mixers, i // n_mixers
        ml = mod[i, :bsz]
        sh1, sc1, g1, sh2, sc2, g2 = [ml[:, k * d:(k + 1) * d][:, None, :] for k in range(6)]
        bias, scale = zeros, ones
        if kind == 0:
            mc = mod[i, bsz]
            a = _attention_mixer(xl, ctx, sc1, sh1, mc[d:2 * d], mc[:d], attn_wqkv[j], attn_lam[j],
                                 attn_subln[j], i)
            w = attn_wo[j]
        elif kind == 1:
            a = _pool_call(xl, sc1, sh1)
            w = jax.scipy.linalg.block_diag(*[pool_w[j, g] for g in range(pool_w.shape[1])])
            scale = pool_scale[j].reshape(1, d)
        elif kind == 2:
            a = _fourier_mixer(xl, sc1, sh1)
            w, bias = fnet_w[j], fnet_b[j].reshape(1, d)
        else:
            a = _hyena_mixer(xl, sc1, sh1, hy_win[j], hy_bin[j], hy_conv_w[j], hy_conv_b[j], hy_f1[j],
                             hy_fb1[j], hy_f2[j], hy_fb2[j], hy_f3[j], hy_freq[j], hy_skip[j])
            w, bias = hy_wo[j], hy_bo[j].reshape(1, d)
        x1, tokx, ri, cnt = _post_call(a, w.astype(BF16), bias, scale, xl, g1, ln_g[i, 0:1], ln_b[i, 0:1],
                                       sc2, sh2, rwt, rb, alpha)
        xl = _moe(x1, tokx, ri, cnt, g2, ln_g[i, 1:2], ln_b[i, 1:2], moe_wg[i].astype(BF16),
                  moe_wu[i].astype(BF16), moe_wd[i].astype(BF16), alpha)
    return xl
```

```python
import functools
import math

import numpy as np
import jax
import jax.numpy as jnp
from jax import lax
from jax.experimental import pallas as pl
from jax.experimental.pallas import tpu as pltpu

F32 = jnp.float32
BF16 = jnp.bfloat16
I32 = jnp.int32

GRID_W = 64
ATT_HEADS = 8
ROPE_THETA = 10000.0
POOL_WINDOWS = (2, 4, 8, 16)
FFT_GROUPS = 4
HY_BANDS = 16
HY_TARGET = 1e-2
HY_FAST = 0.3
HY_SLOW = 1.5
N_GROUPS = 4
EXPERTS_PER_GROUP = 4
LN_EPS = 1e-5

LANES = 128
SUBLANES = 8
MXU_DIM = 256
VMEM_LIMIT = 56 * 1024 * 1024

ROW_TILE = 512
MOE_BLOCK = 256
N_PAIR_CLASSES = 24
CLASS_ROWS = 32
ATT_TQ = 256
ATT_TK = 512
HALO = 16


def _cparams(sem):
    return pltpu.CompilerParams(dimension_semantics=sem, vmem_limit_bytes=VMEM_LIMIT)


def _mm(a, b):
    return jnp.dot(a, b, preferred_element_type=F32)


def _mm_nt(a, b):
    return lax.dot_general(a, b, (((1,), (1,)), ((), ())), preferred_element_type=F32)


def _silu(x):
    return x * jax.nn.sigmoid(x)


def _layer_norm(u, g, b):
    mu = jnp.mean(u, axis=-1, keepdims=True)
    d = u - mu
    var = jnp.mean(d * d, axis=-1, keepdims=True)
    return d * lax.rsqrt(var + LN_EPS) * g + b


def _mod_kernel(c_ref, w_ref, b_ref, o_ref):
    s = _silu(c_ref[...]).astype(BF16)
    o_ref[0] = _mm(s, w_ref[0].astype(BF16)) + b_ref[0]


def _mod_call(c8, mod_w, mod_b):
    depth, d, n = mod_w.shape
    tn = 1536
    return pl.pallas_call(
        _mod_kernel,
        out_shape=jax.ShapeDtypeStruct((depth, 8, n), F32),
        grid=(depth, n // tn),
        in_specs=[
            pl.BlockSpec((8, d), lambda i, j: (0, 0)),
            pl.BlockSpec((1, d, tn), lambda i, j: (i, 0, j)),
            pl.BlockSpec((1, 1, tn), lambda i, j: (i, 0, j)),
        ],
        out_specs=pl.BlockSpec((1, 8, tn), lambda i, j: (i, 0, j)),
        compiler_params=_cparams(("arbitrary", "arbitrary")),
        name="mod",
    )(c8, mod_w, mod_b.reshape(depth, 1, n))


def _route_rows(logits_t):
    m = jnp.max(logits_t, axis=0, keepdims=True)
    e = jnp.exp(logits_t - m)
    p = e / jnp.sum(e, axis=0, keepdims=True)
    r = [p[i:i + 1, :] for i in range(N_GROUPS * EXPERTS_PER_GROUP)]
    gs = []
    for g in range(N_GROUPS):
        q = r[4 * g:4 * g + 4]
        best = q[0] + q[1]
        for (i, j) in ((0, 2), (0, 3), (1, 2), (1, 3), (2, 3)):
            best = jnp.maximum(best, q[i] + q[j])
        gs.append(best)
    gbest, gidx = gs[0], jnp.zeros_like(gs[0], dtype=I32)
    for g in range(1, N_GROUPS):
        upd = gs[g] > gbest
        gidx = jnp.where(upd, g, gidx)
        gbest = jnp.where(upd, gs[g], gbest)
    v = []
    for j in range(EXPERTS_PER_GROUP):
        vj = r[j]
        for g in range(1, N_GROUPS):
            vj = jnp.where(gidx == g, r[4 * g + j], vj)
        v.append(vj)

    def first_argmax(vals):
        best, idx = vals[0], jnp.zeros_like(gidx)
        for j in range(1, len(vals)):
            upd = vals[j] > best
            idx = jnp.where(upd, j, idx)
            best = jnp.where(upd, vals[j], best)
        return best, idx

    v1, i1 = first_argmax(v)
    v2, i2 = first_argmax([jnp.where(i1 == j, -1.0, v[j]) for j in range(EXPERTS_PER_GROUP)])
    tot = v1 + v2
    w1, w2 = v1 / tot, v2 / tot
    lo = jnp.minimum(i1, i2)
    hi = jnp.maximum(i1, i2)
    pair = jnp.where(lo == 0, hi - 1, jnp.where(lo == 1, hi + 1, 5))
    cls = gidx * 6 + pair
    first_is_lo = i1 < i2
    w_lo = jnp.where(first_is_lo, w1, w2)
    w_hi = jnp.where(first_is_lo, w2, w1)
    return cls, w_lo, w_hi


def _post_kernel(a_ref, w_ref, bias_ref, scale_ref, x_ref, g1_ref, lng_ref, lnb_ref, sc2_ref, sh2_ref,
                 rwt_ref, rb_ref, tri_ref, x1_ref, tokx_ref, ri_ref, cnt_ref, cnt_scr, *, alpha, d):
    first = jnp.logical_and(pl.program_id(0) == 0, pl.program_id(1) == 0)

    @pl.when(first)
    def _():
        cnt_scr[...] = jnp.zeros_like(cnt_scr)

    y = _mm(a_ref[...].astype(BF16), w_ref[...])
    y = (y + bias_ref[...]) * scale_ref[...]
    u = alpha * x_ref[...] + (1.0 + g1_ref[...]) * y
    x1 = _layer_norm(u, lng_ref[...], lnb_ref[...])
    x1_ref[...] = x1
    tok = x1 * (1.0 + sc2_ref[...]) + sh2_ref[...]
    tokx_ref[:, :d] = tok
    tm = tok.shape[0]
    logits_t = _mm_nt(rwt_ref[...], tok.astype(BF16)) + rb_ref[...]
    cls, w_lo, w_hi = _route_rows(logits_t)
    crow = lax.broadcasted_iota(I32, (CLASS_ROWS, tm), 0)
    onehot = (crow == cls).astype(F32)
    prefix = _mm(onehot.astype(BF16), tri_ref[...])
    base = cnt_scr[...]
    rank = jnp.sum(onehot * (prefix + base), axis=0, keepdims=True).astype(I32)
    cnt_new = base + jnp.sum(onehot, axis=1, keepdims=True)
    cnt_scr[...] = cnt_new
    cnt_ref[...] = cnt_new[:, :LANES]
    r8 = lax.broadcasted_iota(I32, (SUBLANES, tm), 0)
    ri_ref[...] = jnp.where(r8 == 0, cls, jnp.where(r8 == 1, rank, 0))
    r128 = lax.broadcasted_iota(I32, (LANES, tm), 0)
    wmat = jnp.where(r128 == 0, w_lo, jnp.where(r128 == 1, w_hi, 0.0))
    tokx_ref[:, d:] = wmat.T


def _post_call(a, w, bias, scale, x, g1, lng, lnb, sc2, sh2, rwt, rb, alpha):
    bsz, l, d = x.shape
    k = a.shape[-1]
    tm = min(ROW_TILE, l)
    nt = l // tm
    t = bsz * l
    tri = jnp.asarray(np.triu(np.ones((tm, tm), np.float32), 1), BF16)
    vec = lambda: pl.BlockSpec((1, d), lambda b, i: (0, 0))
    bvec = lambda: pl.BlockSpec((None, 1, d), lambda b, i: (b, 0, 0))
    return pl.pallas_call(
        functools.partial(_post_kernel, alpha=alpha, d=d),
        out_shape=(
            jax.ShapeDtypeStruct((bsz, l, d), F32),
            jax.ShapeDtypeStruct((t, d + LANES), F32),
            jax.ShapeDtypeStruct((SUBLANES, t), I32),
            jax.ShapeDtypeStruct((CLASS_ROWS, LANES), F32),
        ),
        grid=(bsz, nt),
        in_specs=[
            pl.BlockSpec((None, tm, k), lambda b, i: (b, i, 0)),
            pl.BlockSpec((k, d), lambda b, i: (0, 0)),
            vec(), vec(),
            pl.BlockSpec((None, tm, d), lambda b, i: (b, i, 0)),
            bvec(), vec(), vec(), bvec(), bvec(),
            pl.BlockSpec(rwt.shape, lambda b, i: (0, 0)),
            pl.BlockSpec(rb.shape, lambda b, i: (0, 0)),
            pl.BlockSpec((tm, tm), lambda b, i: (0, 0)),
        ],
        out_specs=(
            pl.BlockSpec((None, tm, d), lambda b, i: (b, i, 0)),
            pl.BlockSpec((tm, d + LANES), lambda b, i: (b * nt + i, 0)),
            pl.BlockSpec((SUBLANES, tm), lambda b, i: (0, b * nt + i)),
            pl.BlockSpec((CLASS_ROWS, LANES), lambda b, i: (0, 0)),
        ),
        scratch_shapes=[pltpu.VMEM((CLASS_ROWS, tm), F32)],
        compiler_params=_cparams(("arbitrary", "arbitrary")),
        name="post",
    )(a, w, bias, scale, x, g1, lng, lnb, sc2, sh2, rwt, rb, tri)


def _invperm_kernel(dest_ref, src_ref, *, t, p, chunk):
    phase = pl.program_id(0)
    base = pl.program_id(1) * chunk

    @pl.when(jnp.logical_and(phase == 0, base < p))
    def _():
        def fill(r, c):
            src_ref[base + r] = t + r
            return c
        lax.fori_loop(0, chunk, fill, 0, unroll=8)

    @pl.when(jnp.logical_and(phase == 1, base < t))
    def _():
        def put(i, c):
            src_ref[dest_ref[base + i]] = base + i
            return c
        lax.fori_loop(0, chunk, put, 0, unroll=8)


def _invperm_call(dest, n_rows):
    t = dest.shape[0]
    chunk = 2 * MOE_BLOCK
    assert n_rows % MOE_BLOCK == 0 and t % chunk == 0
    n_chunks = -(-n_rows // chunk)
    return pl.pallas_call(
        functools.partial(_invperm_kernel, t=t, p=n_rows, chunk=chunk),
        out_shape=jax.ShapeDtypeStruct((n_chunks * chunk,), I32),
        grid=(2, n_chunks),
        in_specs=[pl.BlockSpec(memory_space=pltpu.SMEM)],
        out_specs=pl.BlockSpec(memory_space=pltpu.SMEM),
        compiler_params=_cparams(("arbitrary", "arbitrary")),
        name="moe_invperm",
    )(dest)


def _expert_kernel(src_ref, ea_ref, eb_ref, nu_ref, tok_hbm, wga_ref, wua_ref, wda_ref, wgb_ref, wub_ref,
                   wdb_ref, y_hbm, xbuf, ybuf, gsem, ssem, *, d, fc, bm, t):
    del ea_ref, eb_ref
    b = pl.program_id(0)
    nu = nu_ref[0]
    slot = b % 2

    def gather_row(blk, s, r):
        tok = jnp.minimum(src_ref[blk * bm + r], t - 1)
        return pltpu.make_async_copy(tok_hbm.at[pl.ds(tok, 1)], xbuf.at[s, pl.ds(r, 1)], gsem.at[s])

    def scatter_row(blk, s, r):
        return pltpu.make_async_copy(ybuf.at[s, pl.ds(r, 1)], y_hbm.at[pl.ds(src_ref[blk * bm + r], 1)],
                                     ssem.at[s])

    def wait_gather(s):
        pltpu.make_async_copy(tok_hbm.at[pl.ds(0, bm)], xbuf.at[s], gsem.at[s]).wait()

    def wait_scatter(s):
        pltpu.make_async_copy(ybuf.at[s], y_hbm.at[pl.ds(0, bm)], ssem.at[s]).wait()

    @pl.when(b == 0)
    def _():
        ybuf[...] = jnp.zeros_like(ybuf)
        for s in range(2):
            cp = pltpu.make_async_copy(ybuf.at[s], y_hbm.at[pl.ds(t + s * bm, bm)], ssem.at[s])
            cp.start()
            cp.wait()
        for r in range(bm):
            gather_row(0, 0, r).start()

    @pl.when(b < nu)
    def _():
        wait_gather(slot)

        @pl.when(b >= 2)
        def _():
            wait_scatter(slot)

        nxt = jnp.minimum(b + 1, nu - 1)
        for r in range(bm):
            gather_row(nxt, 1 - slot, r).start()

        x = xbuf[slot, :, :d].astype(BF16)
        wts = xbuf[slot, :, d:]
        w_lo = wts[:, 0:1]
        w_hi = wts[:, 1:2]

        def ffn(wg_ref, wu_ref, wd_ref):
            f = wg_ref.shape[1]
            acc = None
            for c in range(f // fc):
                g = _mm(x, wg_ref[:, c * fc:(c + 1) * fc])
                u = _mm(x, wu_ref[:, c * fc:(c + 1) * fc])
                h = (_silu(g) * u).astype(BF16)
                part = _mm(h, wd_ref[c * fc:(c + 1) * fc, :])
                acc = part if acc is None else acc + part
            return acc

        ybuf[slot] = w_lo * ffn(wga_ref, wua_ref, wda_ref) + w_hi * ffn(wgb_ref, wub_ref, wdb_ref)
        for r in range(bm):
            scatter_row(b, slot, r).start()

        @pl.when(b == nu - 1)
        def _():
            wait_gather(1 - slot)
            wait_scatter(slot)

            @pl.when(b >= 1)
            def _():
                wait_scatter(1 - slot)


def _expert_call(src, blk_ea, blk_eb, n_used, tokx, wg, wu, wd):
    t, dx = tokx.shape
    d = dx - LANES
    f = wg.shape[-1]
    bm = MOE_BLOCK
    n_blocks = blk_ea.shape[0]
    amap = lambda b, s, ea, eb, nu: (ea[b], 0, 0)
    bmap = lambda b, s, ea, eb, nu: (eb[b], 0, 0)
    return pl.pallas_call(
        functools.partial(_expert_kernel, d=d, fc=512, bm=bm, t=t),
        out_shape=jax.ShapeDtypeStruct((t + 2 * bm, d), F32),
        grid_spec=pltpu.PrefetchScalarGridSpec(
            num_scalar_prefetch=4,
            grid=(n_blocks,),
            in_specs=[
                pl.BlockSpec(memory_space=pl.ANY),
                pl.BlockSpec((None, d, f), amap), pl.BlockSpec((None, d, f), amap), pl.BlockSpec((None, f, d), amap),
                pl.BlockSpec((None, d, f), bmap), pl.BlockSpec((None, d, f), bmap), pl.BlockSpec((None, f, d), bmap),
            ],
            out_specs=pl.BlockSpec(memory_space=pl.ANY),
            scratch_shapes=[pltpu.VMEM((2, bm, dx), F32), pltpu.VMEM((2, bm, d), F32),
                            pltpu.SemaphoreType.DMA((2,)), pltpu.SemaphoreType.DMA((2,))],
        ),
        compiler_params=_cparams(("arbitrary",)),
        name="experts",
    )(src, blk_ea, blk_eb, n_used, tokx, wg, wu, wd, wg, wu, wd)


def _combine_kernel(y2_ref, x1_ref, g2_ref, lng_ref, lnb_ref, o_ref, *, alpha):
    u = alpha * x1_ref[...] + (1.0 + g2_ref[...]) * y2_ref[...]
    o_ref[...] = _layer_norm(u, lng_ref[...], lnb_ref[...])


def _combine_call(y2, x1, g2, lng, lnb, alpha):
    bsz, l, d = x1.shape
    tm = min(ROW_TILE, l)
    nt = l // tm
    return pl.pallas_call(
        functools.partial(_combine_kernel, alpha=alpha),
        out_shape=jax.ShapeDtypeStruct((bsz, l, d), F32),
        grid=(bsz, nt),
        in_specs=[
            pl.BlockSpec((tm, d), lambda b, i: (b * nt + i, 0)),
            pl.BlockSpec((None, tm, d), lambda b, i: (b, i, 0)),
            pl.BlockSpec((None, 1, d), lambda b, i: (b, 0, 0)),
            pl.BlockSpec((1, d), lambda b, i: (0, 0)),
            pl.BlockSpec((1, d), lambda b, i: (0, 0)),
        ],
        out_specs=pl.BlockSpec((None, tm, d), lambda b, i: (b, i, 0)),
        compiler_params=_cparams(("arbitrary", "arbitrary")),
        name="combine",
    )(y2, x1, g2, lng, lnb)


def _pair_tables():
    ta, tb = [], []
    for g in range(N_GROUPS):
        for a in range(EXPERTS_PER_GROUP):
            for b in range(a + 1, EXPERTS_PER_GROUP):
                ta.append(g * EXPERTS_PER_GROUP + a)
                tb.append(g * EXPERTS_PER_GROUP + b)
    return np.asarray(ta, np.int32), np.asarray(tb, np.int32)


def _moe(x1, tokx, ri, cnt, g2, lng, lnb, wg, wu, wd, alpha):
    bsz, l, d = x1.shape
    t = bsz * l
    bm = MOE_BLOCK
    n_blocks = t // bm + N_PAIR_CLASSES
    counts = cnt[:N_PAIR_CLASSES, 0].astype(I32)
    nblk = (counts + bm - 1) // bm
    cend = jnp.cumsum(nblk)
    cstart = cend - nblk
    dest = cstart[ri[0]] * bm + ri[1]
    blk = jnp.arange(n_blocks, dtype=I32)
    blk_cls = jnp.minimum(jnp.sum((cend[None, :] <= blk[:, None]).astype(I32), axis=1), N_PAIR_CLASSES - 1)
    ta, tb = _pair_tables()
    blk_ea = jnp.asarray(ta)[blk_cls]
    blk_eb = jnp.asarray(tb)[blk_cls]
    n_used = cend[-1:].astype(I32)
    src = _invperm_call(dest, n_blocks * bm)
    y2 = _expert_call(src, blk_ea, blk_eb, n_used, tokx, wg, wu, wd)
    return _combine_call(y2, x1, g2, lng, lnb, alpha)


def _qkv_kernel(x_ref, sc_ref, sh_ref, w_ref, *rest, rope, groups, d, q_scale):
    if rope:
        tab_ref, out_refs = rest[0], rest[1:]
    else:
        tab_ref, out_refs = None, rest
    h = (x_ref[...] * (1.0 + sc_ref[...]) + sh_ref[...]).astype(BF16)
    for (g, transposed), o_ref in zip(groups, out_refs):
        for c in range(d // MXU_DIM):
            col = g * d + c * MXU_DIM
            z = _mm(h, w_ref[:, col:col + MXU_DIM])
            if rope and g < 2:
                heads = []
                for hh in range(MXU_DIM // LANES):
                    zh = z[:, hh * LANES:(hh + 1) * LANES]
                    heads.append(zh * tab_ref[0] + pltpu.roll(zh, LANES - 16, 1) * tab_ref[1]
                                 + pltpu.roll(zh, 16, 1) * tab_ref[2])
                z = jnp.concatenate(heads, axis=1)
            if g == 0:
                z = z * q_scale
            if transposed:
                o_ref[c * MXU_DIM:(c + 1) * MXU_DIM, :] = z.T.astype(BF16)
            else:
                o_ref[:, c * MXU_DIM:(c + 1) * MXU_DIM] = z.astype(BF16)


def _qkv_call(x, sc, sh, w, tab, groups, q_scale):
    bsz, l, d = x.shape
    tm = min(ROW_TILE, l)
    rope = tab is not None
    in_specs = [
        pl.BlockSpec((None, tm, d), lambda b, i: (b, i, 0)),
        pl.BlockSpec((None, 1, d), lambda b, i: (b, 0, 0)),
        pl.BlockSpec((None, 1, d), lambda b, i: (b, 0, 0)),
        pl.BlockSpec(w.shape, lambda b, i: (0, 0)),
    ]
    args = [x, sc, sh, w]
    if rope:
        in_specs.append(pl.BlockSpec((3, tm, LANES), lambda b, i: (0, i, 0)))
        args.append(tab)
    out_shape, out_specs = [], []
    for (_, transposed) in groups:
        if transposed:
            out_shape.append(jax.ShapeDtypeStruct((bsz, d, l), BF16))
            out_specs.append(pl.BlockSpec((None, d, tm), lambda b, i: (b, 0, i)))
        else:
            out_shape.append(jax.ShapeDtypeStruct((bsz, l, d), BF16))
            out_specs.append(pl.BlockSpec((None, tm, d), lambda b, i: (b, i, 0)))
    return pl.pallas_call(
        functools.partial(_qkv_kernel, rope=rope, groups=groups, d=d, q_scale=q_scale),
        out_shape=tuple(out_shape),
        grid=(bsz, l // tm),
        in_specs=in_specs,
        out_specs=tuple(out_specs),
        compiler_params=_cparams(("arbitrary", "arbitrary")),
        name="qkv",
    )(*args)


def _attn_kernel(q_ref, kt_ref, v_ref, kct_ref, vc_ref, lam_ref, subln_ref, o_ref, m_scr, acc_scr, s_scr, p_scr,
                 a_scr, *, tq, tk, lam_init):
    q = q_ref[...]
    hd = q.shape[1]
    lane = lax.broadcasted_iota(I32, q.shape, 1)
    zero = jnp.zeros_like(q)
    lhs = jnp.concatenate([jnp.where(lane < hd // 2, q, zero), jnp.where(lane >= hd // 2, q, zero)], axis=0)
    m_scr[...] = jnp.full_like(m_scr, -jnp.inf)
    acc_scr[...] = jnp.zeros_like(acc_scr)

    def softmax(s):
        m_old = m_scr[...]
        m_new = jnp.maximum(m_old, jnp.max(s, axis=1, keepdims=True))
        m_scr[...] = m_new
        p = jnp.exp2(s - jnp.concatenate([m_new] * (s.shape[1] // LANES), axis=1)).astype(BF16)
        return p, jnp.exp2(m_old - m_new)

    def accumulate(p, a, vb):
        tkk = vb.shape[0]
        ones_col = jnp.where(lax.broadcasted_iota(I32, (tkk, LANES), 1) == 0, 1.0, 0.0).astype(BF16)
        vext = jnp.concatenate([vb, ones_col], axis=1)
        acc_scr[...] = jnp.concatenate([a, a], axis=1) * acc_scr[...] + _mm(p, vext)

    def chunk_start(j):
        return j * tk if isinstance(j, int) else pl.multiple_of(j * tk, tk)

    def k_chunk(j):
        return kt_ref[:, pl.ds(chunk_start(j), tk)]

    def v_chunk(j):
        return v_ref[pl.ds(chunk_start(j), tk), :]

    n_chunks = v_ref.shape[0] // tk
    s_scr[0] = _mm(lhs, k_chunk(0))
    if n_chunks > 1:
        s_scr[1] = _mm(lhs, k_chunk(1))
    p0, a0 = softmax(s_scr[0])
    p_scr[0] = p0
    a_scr[0] = a0

    def stage(j, slot):
        s_next = _mm(lhs, k_chunk(j + 1))
        accumulate(p_scr[1 - slot], a_scr[1 - slot], v_chunk(j - 1))
        p, a = softmax(s_scr[slot])
        s_scr[1 - slot] = s_next
        p_scr[slot] = p
        a_scr[slot] = a

    for j in range(1, n_chunks - 1):
        stage(j, j % 2)
    last = (n_chunks - 1) % 2
    sc = _mm(lhs, kct_ref[...])
    if n_chunks > 1:
        accumulate(p_scr[1 - last], a_scr[1 - last], v_chunk(n_chunks - 2))
        p, a = softmax(s_scr[last])
    else:
        p, a = p0, a0
    accumulate(p, a, v_chunk(n_chunks - 1))
    pc, ac = softmax(sc)
    accumulate(pc, ac, vc_ref[...])

    lam = lam_ref[...]
    lam_full = (jnp.exp(jnp.sum(lam[0:1] * lam[1:2], axis=1, keepdims=True))
                - jnp.exp(jnp.sum(lam[2:3] * lam[3:4], axis=1, keepdims=True)) + lam_init)
    acc = acc_scr[...]
    o12 = acc[:, :hd] / acc[:, hd:hd + 1]
    o = o12[:tq] - lam_full * o12[tq:]
    o = o * lax.rsqrt(jnp.mean(o * o, axis=1, keepdims=True) + LN_EPS)
    o = o * (subln_ref[...] * (1.0 - lam_init))
    o_ref[...] = o.astype(BF16)


def _attn_call(q, kt, v, kct, vc, lam, subln_row, lam_init):
    bsz, l, d = q.shape
    lc = vc.shape[1]
    hd = d // ATT_HEADS
    tq = min(ATT_TQ, l)
    tk = min(ATT_TK, l)
    return pl.pallas_call(
        functools.partial(_attn_kernel, tq=tq, tk=tk, lam_init=lam_init),
        out_shape=jax.ShapeDtypeStruct((bsz, l, d), BF16),
        grid=(bsz, ATT_HEADS, l // tq),
        in_specs=[
            pl.BlockSpec((None, tq, hd), lambda b, h, i: (b, i, h)),
            pl.BlockSpec((None, hd, l), lambda b, h, i: (b, h, 0)),
            pl.BlockSpec((None, l, hd), lambda b, h, i: (b, 0, h)),
            pl.BlockSpec((None, hd, lc), lambda b, h, i: (b, h, 0)),
            pl.BlockSpec((None, lc, hd), lambda b, h, i: (b, 0, h)),
            pl.BlockSpec(lam.shape, lambda b, h, i: (0, 0)),
            pl.BlockSpec(subln_row.shape, lambda b, h, i: (0, 0)),
        ],
        out_specs=pl.BlockSpec((None, tq, hd), lambda b, h, i: (b, i, h)),
        scratch_shapes=[pltpu.VMEM((2 * tq, LANES), F32), pltpu.VMEM((2 * tq, hd + LANES), F32),
                        pltpu.VMEM((2, 2 * tq, tk), F32), pltpu.VMEM((2, 2 * tq, tk), BF16),
                        pltpu.VMEM((2, 2 * tq, LANES), F32)],
        compiler_params=_cparams(("arbitrary", "arbitrary", "arbitrary")),
        name="diff_attn",
    )(q, kt, v, kct, vc, lam, subln_row)


def _rope_tables(l, hd):
    dh = hd // 2
    half = dh // 2
    rows = l // GRID_W
    row = jnp.repeat(jnp.arange(rows), GRID_W).astype(F32)
    col = jnp.tile(jnp.arange(GRID_W), rows).astype(F32)
    inv = ROPE_THETA ** (-jnp.arange(0, half, 2, dtype=F32) / half)
    ar = row[:, None] * inv
    ac = col[:, None] * inv
    ang = jnp.concatenate([ar, ar, ac, ac], axis=-1)
    cos, sin = jnp.cos(ang), jnp.sin(ang)
    first = (np.arange(dh) % half) < half // 2
    sin_a = jnp.where(first, -sin, 0.0)
    sin_b = jnp.where(first, 0.0, sin)
    return jnp.stack([jnp.tile(cos, (1, 2)), jnp.tile(sin_a, (1, 2)), jnp.tile(sin_b, (1, 2))])


def _attention_mixer(x, ctx, sc1, sh1, csc1, csh1, wqkv, lam, subln, layer_idx):
    bsz, l, d = x.shape
    hd = d // ATT_HEADS
    lam_init = 0.8 - 0.6 * math.exp(-0.3 * layer_idx)
    w = wqkv.astype(BF16)
    q, kt, v = _qkv_call(x, sc1, sh1, w, _rope_tables(l, hd), ((0, False), (1, True), (2, False)),
                         (hd // 2) ** -0.5 * math.log2(math.e))
    ones = jnp.ones((bsz, 1, 1), F32)
    kct, vc = _qkv_call(ctx, csc1[None, None, :] * ones, csh1[None, None, :] * ones, w, None,
                        ((1, True), (2, False)), 1.0)
    return _attn_call(q, kt, v, kct, vc, lam, subln.reshape(1, hd), lam_init)


def _halo_specs(tm, d, l):
    per = tm // HALO
    last = l // HALO - 1
    return [
        pl.BlockSpec((None, tm, d), lambda b, i: (b, i, 0)),
        pl.BlockSpec((None, HALO, d), lambda b, i: (b, jnp.maximum(i * per - 1, 0), 0)),
        pl.BlockSpec((None, HALO, d), lambda b, i: (b, jnp.minimum((i + 1) * per, last), 0)),
    ]


def _pool_kernel(x_ref, xp_ref, xn_ref, sc_ref, sh_ref, o_ref, ext, *, tm, l):
    i = pl.program_id(1)
    nt = pl.num_programs(1)
    sc = 1.0 + sc_ref[...]
    sh = sh_ref[...]
    ext[HALO:HALO + tm, :] = x_ref[...] * sc + sh
    ext[0:HALO, :] = jnp.where(i > 0, xp_ref[...] * sc + sh, 0.0)
    ext[HALO + tm:, :] = jnp.where(i < nt - 1, xn_ref[...] * sc + sh, 0.0)
    gw = x_ref.shape[1] // len(POOL_WINDOWS)
    t = i * tm + lax.broadcasted_iota(I32, (tm, gw), 0)
    for g, win in enumerate(POOL_WINDOWS):
        c0 = g * gw
        acc = None
        for j in range(-(win // 2), win - win // 2):
            part = ext[HALO + j:HALO + j + tm, c0:c0 + gw]
            acc = part if acc is None else acc + part
        lo = jnp.maximum(t - win // 2, 0)
        hi = jnp.minimum(t - win // 2 + win, l)
        mean = acc / (hi - lo).astype(F32)
        o_ref[:, c0:c0 + gw] = (mean - ext[HALO:HALO + tm, c0:c0 + gw]).astype(BF16)


def _pool_call(x, sc1, sh1):
    bsz, l, d = x.shape
    tm = min(ROW_TILE, l)
    bvec = lambda: pl.BlockSpec((None, 1, d), lambda b, i: (b, 0, 0))
    return pl.pallas_call(
        functools.partial(_pool_kernel, tm=tm, l=l),
        out_shape=jax.ShapeDtypeStruct((bsz, l, d), BF16),
        grid=(bsz, l // tm),
        in_specs=_halo_specs(tm, d, l) + [bvec(), bvec()],
        out_specs=pl.BlockSpec((None, tm, d), lambda b, i: (b, i, 0)),
        scratch_shapes=[pltpu.VMEM((tm + 2 * HALO, d), F32)],
        compiler_params=_cparams(("arbitrary", "arbitrary")),
        name="pool",
    )(x, x, x, sc1, sh1)


KRON = SUBLANES
INNER = 128
N_RB = INNER // KRON


def _trig(num, den):
    ang = jnp.asarray(np.mod(num, den), F32) * (2.0 * math.pi / den)
    return jnp.cos(ang), jnp.sin(ang)


def _kron_rows(m):
    p, q = m.shape
    rep_r = jnp.asarray(np.kron(np.eye(p), np.ones((KRON, 1))), F32)
    rep_c = jnp.asarray(np.kron(np.eye(q), np.ones((1, KRON))), F32)
    hi = lax.Precision.HIGHEST
    big = jnp.dot(jnp.dot(rep_r, m, precision=hi), rep_c, precision=hi)
    r = lax.broadcasted_iota(I32, big.shape, 0) % KRON
    c = lax.broadcasted_iota(I32, big.shape, 1) % KRON
    return jnp.where(r == c, big, 0.0)


def _outer_kernel(*refs, n_in):
    m_ref, x_refs, o_ref = refs[0], refs[1:1 + n_in], refs[1 + n_in]
    acc = None
    col = 0
    for x_ref in x_refs:
        a, j, d = x_ref.shape
        x = x_ref[...].reshape(a * j, d).astype(BF16)
        part = _mm(m_ref[:, col:col + a * j], x)
        acc = part if acc is None else acc + part
        col += a * j
    o_ref[...] = acc.astype(o_ref.dtype)


def _outer_fwd_call(m, xs, a_out):
    bsz, rows, d = xs[0].shape
    a_in = rows // INNER
    xs4 = [x.reshape(bsz, a_in, N_RB, KRON, d) for x in xs]
    return pl.pallas_call(
        functools.partial(_outer_kernel, n_in=len(xs)),
        out_shape=jax.ShapeDtypeStruct((bsz, N_RB, a_out * 2 * KRON, d), BF16),
        grid=(bsz, N_RB),
        in_specs=[pl.BlockSpec(m.shape, lambda b, r: (0, 0))]
        + [pl.BlockSpec((None, a_in, None, KRON, d), lambda b, r: (b, 0, r, 0, 0)) for _ in xs],
        out_specs=pl.BlockSpec((None, None, a_out * 2 * KRON, d), lambda b, r: (b, r, 0, 0)),
        compiler_params=_cparams(("arbitrary", "arbitrary")),
        name="dft_outer",
    )(m, *xs4)


def _fn_chan_kernel(x_ref, sc_ref, sh_ref, cs_ref, pr_ref, pi_ref, *, gw):
    h = (x_ref[...] * (1.0 + sc_ref[...]) + sh_ref[...]).astype(BF16)
    for g in range(h.shape[1] // gw):
        z = _mm(h[:, g * gw:(g + 1) * gw], cs_ref[...])
        pr_ref[:, g * gw:(g + 1) * gw] = z[:, :gw]
        pi_ref[:, g * gw:(g + 1) * gw] = z[:, gw:]


def _fn_chan_call(x, sc1, sh1):
    bsz, l, d = x.shape
    gw = d // FFT_GROUPS
    tm = min(ROW_TILE, l)
    cc, ss = _trig(np.outer(np.arange(gw), np.arange(gw)), gw)
    cs = jnp.concatenate([cc, -ss], axis=1).astype(BF16)
    bvec = lambda: pl.BlockSpec((None, 1, d), lambda b, i: (b, 0, 0))
    tile = lambda: pl.BlockSpec((None, tm, d), lambda b, i: (b, i, 0))
    return pl.pallas_call(
        functools.partial(_fn_chan_kernel, gw=gw),
        out_shape=(jax.ShapeDtypeStruct((bsz, l, d), F32), jax.ShapeDtypeStruct((bsz, l, d), F32)),
        grid=(bsz, l // tm),
        in_specs=[tile(), bvec(), bvec(), pl.BlockSpec(cs.shape, lambda b, i: (0, 0))],
        out_specs=(tile(), tile()),
        compiler_params=_cparams(("arbitrary", "arbitrary")),
        name="fnet_channel_dft",
    )(x, sc1, sh1, cs)


def _fn_inner_kernel(m_ref, x_ref, o_ref):
    rb, j1, cj, d = x_ref.shape
    x = x_ref[...].reshape(rb * j1 * cj, d)
    y = _mm(m_ref[...], x)
    o_ref[...] = y.reshape(o_ref.shape)


def _fn_inner_call(m2, spec, a):
    bsz, _, _, d = spec.shape
    spec5 = spec.reshape(bsz, N_RB, a, 2 * KRON, d)
    out = pl.pallas_call(
        _fn_inner_kernel,
        out_shape=jax.ShapeDtypeStruct((bsz, INNER, a, d), F32),
        grid=(a // KRON, bsz),
        in_specs=[
            pl.BlockSpec((None,) + m2.shape[1:], lambda s, b: (s, 0, 0)),
            pl.BlockSpec((None, N_RB, KRON, 2 * KRON, d), lambda s, b: (b, 0, s, 0, 0)),
        ],
        out_specs=pl.BlockSpec((None, INNER, KRON, d), lambda s, b: (b, 0, s, 0)),
        compiler_params=_cparams(("arbitrary", "arbitrary")),
        name="fnet_inner_dft",
    )(m2, spec5)
    return out.reshape(bsz, INNER * a, d)


def _fnet_tables(l, gw):
    a = l // INNER
    k1 = np.arange(a)
    c1, s1 = _trig(np.outer(k1, k1), a)
    m1 = jnp.stack([jnp.stack([c1, s1], axis=1), jnp.stack([-s1, c1], axis=1)], axis=1)
    m1k = _kron_rows(m1.reshape(2 * a, 2 * a))
    rb, j1c, cc, jj = np.meshgrid(np.arange(N_RB), np.arange(KRON), np.arange(2), np.arange(KRON), indexing="ij")
    l2 = (rb * KRON + jj).reshape(-1)
    j1c, cc = j1c.reshape(-1), cc.reshape(-1)
    norm = 1.0 / math.sqrt(l * gw)
    ca, sa = _trig(np.outer(np.arange(INNER), l2), INNER)
    p_tab = jnp.where(cc == 0, ca, sa) * norm
    q_tab = jnp.where(cc == 0, -sa, ca) * norm
    k1_col = KRON * np.arange(a // KRON)[:, None] + j1c[None, :]
    cb, sb = _trig(k1_col * l2[None, :], l)
    t3 = p_tab[None] * cb[:, None, :] + q_tab[None] * sb[:, None, :]
    mask = jnp.asarray(j1c[None, :] == np.arange(KRON)[:, None], F32)
    m2 = (t3[:, :, None, :] * mask[None, None]).reshape(a // KRON, INNER * KRON, l2.shape[0])
    return m1k.astype(BF16), m2.astype(BF16)


def _fourier_mixer(x, sc1, sh1):
    bsz, l, d = x.shape
    a = l // INNER
    m1k, m2 = _fnet_tables(l, d // FFT_GROUPS)
    pr, pi = _fn_chan_call(x, sc1, sh1)
    spec = _outer_fwd_call(m1k, [pr, pi], a)
    return _fn_inner_call(m2, spec, a)


def _hy_front_kernel(x_ref, xp_ref, xn_ref, sc_ref, sh_ref, w_ref, bin_ref, cw_ref, cb_ref,
                     vg_ref, x0_ref, hext, zscr, *, tm, l, d):
    i = pl.program_id(1)
    sc = 1.0 + sc_ref[...]
    sh = sh_ref[...]
    hext[HALO:HALO + tm, :] = (x_ref[...] * sc + sh).astype(BF16)
    hext[0:HALO, :] = (xp_ref[...] * sc + sh).astype(BF16)
    hext[HALO + tm:, :] = (xn_ref[...] * sc + sh).astype(BF16)
    pos = i * tm - HALO + lax.broadcasted_iota(I32, (tm + 2 * HALO, MXU_DIM), 0)

    def conv(col):
        z = _mm(hext[...], w_ref[:, col:col + MXU_DIM]) + bin_ref[:, col:col + MXU_DIM]
        zscr[...] = jnp.where(pos >= 0, jnp.where(pos < l, z, 0.0), 0.0)
        cw = cw_ref[:, col:col + MXU_DIM]
        return (zscr[HALO - 1:HALO - 1 + tm, :] * cw[0:1] + zscr[HALO:HALO + tm, :] * cw[1:2]
                + zscr[HALO + 1:HALO + 1 + tm, :] * cw[2:3] + cb_ref[:, col:col + MXU_DIM])

    for c in range(d // MXU_DIM):
        cols = slice(c * MXU_DIM, (c + 1) * MXU_DIM)
        x0_ref[:, cols] = conv(c * MXU_DIM)
        x1 = conv(d + c * MXU_DIM)
        v = conv(2 * d + c * MXU_DIM)
        vg_ref[:, cols] = v * x1


def _hy_front_call(x, sc1, sh1, win, bin_, conv_w, conv_b):
    bsz, l, d = x.shape
    tm = min(ROW_TILE, l)
    n3 = win.shape[1]
    bvec = lambda: pl.BlockSpec((None, 1, d), lambda b, i: (b, 0, 0))
    full = lambda a: pl.BlockSpec(a.shape, lambda b, i: (0, 0))
    tile = lambda: pl.BlockSpec((None, tm, d), lambda b, i: (b, i, 0))
    args = (win.astype(BF16), bin_.reshape(1, n3), conv_w, conv_b.reshape(1, n3))
    return pl.pallas_call(
        functools.partial(_hy_front_kernel, tm=tm, l=l, d=d),
        out_shape=(jax.ShapeDtypeStruct((bsz, l, d), F32), jax.ShapeDtypeStruct((bsz, l, d), F32)),
        grid=(bsz, l // tm),
        in_specs=_halo_specs(tm, d, l) + [bvec(), bvec()] + [full(a) for a in args],
        out_specs=(tile(), tile()),
        scratch_shapes=[pltpu.VMEM((tm + 2 * HALO, d), BF16), pltpu.VMEM((tm + 2 * HALO, MXU_DIM), F32)],
        compiler_params=_cparams(("arbitrary", "arbitrary")),
        name="hyena_front",
    )(x, x, x, sc1, sh1, *args)


def _hy_filter_kernel(z_ref, f1_ref, fb1_ref, f2_ref, fb2_ref, f3_ref, fr_ref, del_ref, o_ref, *, tf, l):
    z = z_ref[...]
    fr = fr_ref[...]
    a = jnp.sin(fr * (_mm(z, f1_ref[...]) + fb1_ref[...]))
    a = jnp.sin(fr * (_mm(a, f2_ref[...]) + fb2_ref[...]))
    k = _mm(a, f3_ref[...])
    window = jnp.exp(-z[:, 0:1] * del_ref[...])
    row = pl.program_id(0) * tf + lax.broadcasted_iota(I32, k.shape, 0)
    o_ref[...] = jnp.where(row == l, 0.0, k * window)


def _hy_filter_call(l, d, f1, fb1, f2, fb2, f3, freq):
    n = 2 * l
    tf = min(ROW_TILE, l)
    hid = f1.shape[1]
    pos = jnp.concatenate([jnp.arange(l, dtype=F32), jnp.zeros((1,), F32),
                           jnp.arange(l - 1, 0, -1, dtype=F32)])
    bands = jnp.linspace(1e-4, HY_BANDS - 1, HY_BANDS, dtype=F32)
    ang = (2.0 * math.pi / l) * pos[:, None] * bands[None, :]
    feat = jnp.concatenate([(pos / l)[:, None], jnp.cos(ang), jnp.sin(ang)], axis=-1)
    kin = 64
    feat = jnp.pad(feat, ((0, 0), (0, kin - feat.shape[1])))
    f1p = jnp.pad(f1, ((0, kin - f1.shape[0]), (0, 0)))
    deltas = jnp.abs(jnp.linspace(math.log(HY_TARGET) / HY_SLOW, math.log(HY_TARGET) / HY_FAST, d,
                                  dtype=F32)).reshape(1, d)
    half = l // tf
    full = lambda a: pl.BlockSpec(a.shape, lambda i: (0, 0))
    args = (f1p, fb1.reshape(1, hid), f2, fb2.reshape(1, hid))
    tail = (freq.reshape(1, hid), jnp.asarray(deltas))
    return pl.pallas_call(
        functools.partial(_hy_filter_kernel, tf=tf, l=l),
        out_shape=jax.ShapeDtypeStruct((n, d), F32),
        grid=(n // tf,),
        in_specs=[pl.BlockSpec((tf, kin), lambda i: (i, 0))] + [full(a) for a in args]
        + [pl.BlockSpec((hid, d), lambda i: (0, i // half))] + [full(a) for a in tail],
        out_specs=pl.BlockSpec((tf, d), lambda i: (i, 0)),
        compiler_params=_cparams(("arbitrary",)),
        name="hyena_filter",
    )(feat, *args, f3, *tail)


def _hy_kspec_kernel(mf_ref, x_ref, o_ref):
    rb, cj, d = x_ref.shape
    o_ref[...] = _mm(mf_ref[...], x_ref[...].reshape(rb * cj, d))


def _hy_kspec_call(mf2, spec, a):
    _, _, _, d = spec.shape
    spec5 = spec.reshape(1, N_RB, a, 2 * KRON, d)
    return pl.pallas_call(
        _hy_kspec_kernel,
        out_shape=jax.ShapeDtypeStruct((a, 2 * INNER, d), F32),
        grid=(a,),
        in_specs=[
            pl.BlockSpec((None, 2 * INNER, 2 * INNER), lambda k: (k, 0, 0)),
            pl.BlockSpec((None, N_RB, None, 2 * KRON, d), lambda k: (0, 0, k, 0, 0)),
        ],
        out_specs=pl.BlockSpec((None, 2 * INNER, d), lambda k: (k, 0, 0)),
        compiler_params=_cparams(("arbitrary",)),
        name="hyena_filter_spectrum",
    )(mf2, spec5)


def _hy_mid_kernel(mf_ref, mi_ref, k_ref, x_ref, o_ref):
    bsz, rb, cj, d = x_ref.shape
    kk = k_ref[...]
    kr, ki = kk[:INNER], kk[INNER:]
    for b in range(bsz):
        v = _mm(mf_ref[...], x_ref[b].reshape(rb * cj, d))
        vr, vi = v[:INNER], v[INNER:]
        z = jnp.concatenate([vr * kr - vi * ki, vr * ki + vi * kr], axis=0).astype(BF16)
        y = _mm(mi_ref[...], z)
        o_ref[b] = y.astype(BF16).reshape(rb, cj, d)


def _hy_mid_call(mf2, mi1, kspec, spec, a):
    bsz, _, _, d = spec.shape
    spec5 = spec.reshape(bsz, N_RB, a, 2 * KRON, d)
    mat = lambda: pl.BlockSpec((None, 2 * INNER, 2 * INNER), lambda k: (k, 0, 0))
    blk = lambda: pl.BlockSpec((bsz, N_RB, None, 2 * KRON, d), lambda k: (0, 0, k, 0, 0))
    out = pl.pallas_call(
        _hy_mid_kernel,
        out_shape=jax.ShapeDtypeStruct(spec5.shape, BF16),
        grid=(a,),
        in_specs=[mat(), mat(), pl.BlockSpec((None, 2 * INNER, d), lambda k: (k, 0, 0)), blk()],
        out_specs=blk(),
        compiler_params=_cparams(("arbitrary",)),
        name="hyena_spectral_product",
    )(mf2, mi1, kspec, spec5)
    return out.reshape(spec.shape)


def _hy_out_kernel(m_ref, b_ref, vg_ref, x0_ref, skip_ref, o_ref):
    a2, j, d = vg_ref.shape
    y = _mm(m_ref[...], b_ref[...])
    vg = vg_ref[...].reshape(a2 * j, d)
    x0 = x0_ref[...].reshape(a2 * j, d)
    o_ref[...] = ((y + vg * skip_ref[...]) * x0).reshape(a2, j, d)


def _hy_out_call(mi2, spec, vg, x0, skip):
    bsz, l, d = vg.shape
    a2 = l // INNER
    view = lambda t: t.reshape(bsz, a2, N_RB, KRON, d)
    blk = lambda: pl.BlockSpec((None, a2, None, KRON, d), lambda b, r: (b, 0, r, 0, 0))
    out = pl.pallas_call(
        _hy_out_kernel,
        out_shape=jax.ShapeDtypeStruct((bsz, a2, N_RB, KRON, d), F32),
        grid=(bsz, N_RB),
        in_specs=[
            pl.BlockSpec(mi2.shape, lambda b, r: (0, 0)),
            pl.BlockSpec((None, None) + spec.shape[2:], lambda b, r: (b, r, 0, 0)),
            blk(), blk(),
            pl.BlockSpec((1, d), lambda b, r: (0, 0)),
        ],
        out_specs=blk(),
        compiler_params=_cparams(("arbitrary", "arbitrary")),
        name="hyena_inverse_outer",
    )(mi2, spec, view(vg), view(x0), skip.reshape(1, d))
    return out.reshape(bsz, l, d)


def _hyena_tables(l):
    n = 2 * l
    a = n // INNER
    kb = np.arange(a)
    cf, sf = _trig(np.outer(kb, kb), a)
    fwd = jnp.stack([cf, -sf], axis=1)
    mf1 = _kron_rows(fwd[:, :, :a // 2].reshape(2 * a, a // 2))
    mk1 = _kron_rows(fwd.reshape(2 * a, a))
    inv = jnp.stack([cf, -sf], axis=2)[:a // 2] * (1.0 / n)
    mi2 = _kron_rows(inv.reshape(a // 2, 2 * a))
    rb, cc, jj = np.meshgrid(np.arange(N_RB), np.arange(2), np.arange(KRON), indexing="ij")
    li = (rb * KRON + jj).reshape(-1)
    cc = cc.reshape(-1)
    ca, sa = _trig(np.outer(np.arange(INNER), li), INNER)
    p_tab = jnp.concatenate([jnp.where(cc == 0, ca, sa), jnp.where(cc == 0, -sa, ca)], axis=0)
    q_tab = jnp.concatenate([jnp.where(cc == 0, -sa, ca), jnp.where(cc == 0, -ca, -sa)], axis=0)
    cb, sb = _trig(np.outer(kb, li), n)
    mf2 = (p_tab[None] * cb[:, None, :] + q_tab[None] * sb[:, None, :]).astype(BF16)
    mi1 = (p_tab.T[None] * cb[:, :, None] + q_tab.T[None] * sb[:, :, None]).astype(BF16)
    return (mf1.astype(BF16), mk1.astype(BF16), mi2.astype(BF16), mf2, mi1)


def _hyena_mixer(x, sc1, sh1, win, bin_, conv_w, conv_b, f1, fb1, f2, fb2, f3, freq, skip):
    bsz, l, d = x.shape
    a = 2 * l // INNER
    mf1, mk1, mi2, mf2, mi1 = _hyena_tables(l)
    vg, x0 = _hy_front_call(x, sc1, sh1, win, bin_, conv_w, conv_b)
    kfull = _hy_filter_call(l, d, f1, fb1, f2, fb2, f3, freq)
    kspec = _hy_kspec_call(mf2, _outer_fwd_call(mk1, [kfull[None]], a), a)
    spec = _outer_fwd_call(mf1, [vg], a)
    spec = _hy_mid_call(mf2, mi1, kspec, spec, a)
    return _hy_out_call(mi2, spec, vg, x0, skip)


def kernel(x, c, ctx, c_ctx, mod_w, mod_b, ln_g, ln_b, attn_wqkv, attn_wo, attn_lam, attn_subln, pool_w,
           pool_scale, fnet_w, fnet_b, hy_win, hy_bin, hy_conv_w, hy_conv_b, hy_f1, hy_fb1, hy_f2, hy_fb2,
           hy_f3, hy_freq, hy_skip, hy_wo, hy_bo, router_w, router_b, moe_wg, moe_wu, moe_wd):
    bsz, l, d = x.shape
    depth = mod_w.shape[0]
    n_mixers = 4
    assert depth <= n_mixers
    assert bsz + 1 <= SUBLANES
    alpha = (2 * depth) ** 0.25
    c8 = jnp.zeros((SUBLANES, d), F32).at[:bsz].set(c).at[bsz].set(c_ctx)
    mod = _mod_call(c8, mod_w, mod_b)
    rwt = router_w.T.astype(BF16)
    rb = router_b.reshape(-1, 1)
    zeros = jnp.zeros((1, d), F32)
    ones = jnp.ones((1, d), F32)
    xl = x
    for i in range(depth):
        kind, j = i % n_mixers, i // n_mixers
        ml = mod[i, :bsz]
        sh1, sc1, g1, sh2, sc2, g2 = [ml[:, k * d:(k + 1) * d][:, None, :] for k in range(6)]
        bias, scale = zeros, ones
        if kind == 0:
            mc = mod[i, bsz]
            a = _attention_mixer(xl, ctx, sc1, sh1, mc[d:2 * d], mc[:d], attn_wqkv[j], attn_lam[j],
                                 attn_subln[j], i)
            w = attn_wo[j]
        elif kind == 1:
            a = _pool_call(xl, sc1, sh1)
            w = jax.scipy.linalg.block_diag(*[pool_w[j, g] for g in range(pool_w.shape[1])])
            scale = pool_scale[j].reshape(1, d)
        elif kind == 2:
            a = _fourier_mixer(xl, sc1, sh1)
            w, bias = fnet_w[j], fnet_b[j].reshape(1, d)
        else:
            a = _hyena_mixer(xl, sc1, sh1, hy_win[j], hy_bin[j], hy_conv_w[j], hy_conv_b[j], hy_f1[j],
                             hy_fb1[j], hy_f2[j], hy_fb2[j], hy_f3[j], hy_freq[j], hy_skip[j])
            w, bias = hy_wo[j], hy_bo[j].reshape(1, d)
        x1, tokx, ri, cnt = _post_call(a, w.astype(BF16), bias, scale, xl, g1, ln_g[i, 0:1], ln_b[i, 0:1],
                                       sc2, sh2, rwt, rb, alpha)
        xl = _moe(x1, tokx, ri, cnt, g2, ln_g[i, 1:2], ln_b[i, 1:2], moe_wg[i].astype(BF16),
                  moe_wu[i].astype(BF16), moe_wd[i].astype(BF16), alpha)
    return xl
```

```python
import functools
import math

import numpy as np
import jax
import jax.numpy as jnp
from jax import lax
from jax.experimental import pallas as pl
from jax.experimental.pallas import tpu as pltpu

F32 = jnp.float32
BF16 = jnp.bfloat16
I32 = jnp.int32

GRID_W = 64
ATT_HEADS = 8
ROPE_THETA = 10000.0
POOL_WINDOWS = (2, 4, 8, 16)
FFT_GROUPS = 4
HY_BANDS = 16
HY_TARGET = 1e-2
HY_FAST = 0.3
HY_SLOW = 1.5
N_GROUPS = 4
EXPERTS_PER_GROUP = 4
LN_EPS = 1e-5

LANES = 128
SUBLANES = 8
MXU_DIM = 256
VMEM_LIMIT = 56 * 1024 * 1024

ROW_TILE = 512
MOE_BLOCK = 256
N_PAIR_CLASSES = 24
CLASS_ROWS = 32
ATT_TQ = 256
ATT_TK = 512
HALO = 16


def _cparams(sem):
    return pltpu.CompilerParams(dimension_semantics=sem, vmem_limit_bytes=VMEM_LIMIT)


def _mm(a, b):
    return jnp.dot(a, b, preferred_element_type=F32)


def _mm_nt(a, b):
    return lax.dot_general(a, b, (((1,), (1,)), ((), ())), preferred_element_type=F32)


def _silu(x):
    return x * jax.nn.sigmoid(x)


def _layer_norm(u, g, b):
    mu = jnp.mean(u, axis=-1, keepdims=True)
    d = u - mu
    var = jnp.mean(d * d, axis=-1, keepdims=True)
    return d * lax.rsqrt(var + LN_EPS) * g + b


def _mod_kernel(c_ref, w_ref, b_ref, o_ref):
    s = _silu(c_ref[...]).astype(BF16)
    o_ref[0] = _mm(s, w_ref[0].astype(BF16)) + b_ref[0]


def _mod_call(c8, mod_w, mod_b):
    depth, d, n = mod_w.shape
    tn = 1536
    return pl.pallas_call(
        _mod_kernel,
        out_shape=jax.ShapeDtypeStruct((depth, 8, n), F32),
        grid=(depth, n // tn),
        in_specs=[
            pl.BlockSpec((8, d), lambda i, j: (0, 0)),
            pl.BlockSpec((1, d, tn), lambda i, j: (i, 0, j)),
            pl.BlockSpec((1, 1, tn), lambda i, j: (i, 0, j)),
        ],
        out_specs=pl.BlockSpec((1, 8, tn), lambda i, j: (i, 0, j)),
        compiler_params=_cparams(("arbitrary", "arbitrary")),
        name="mod",
    )(c8, mod_w, mod_b.reshape(depth, 1, n))


def _route_rows(logits_t):
    m = jnp.max(logits_t, axis=0, keepdims=True)
    e = jnp.exp(logits_t - m)
    p = e / jnp.sum(e, axis=0, keepdims=True)
    r = [p[i:i + 1, :] for i in range(N_GROUPS * EXPERTS_PER_GROUP)]
    gs = []
    for g in range(N_GROUPS):
        q = r[4 * g:4 * g + 4]
        best = q[0] + q[1]
        for (i, j) in ((0, 2), (0, 3), (1, 2), (1, 3), (2, 3)):
            best = jnp.maximum(best, q[i] + q[j])
        gs.append(best)
    gbest, gidx = gs[0], jnp.zeros_like(gs[0], dtype=I32)
    for g in range(1, N_GROUPS):
        upd = gs[g] > gbest
        gidx = jnp.where(upd, g, gidx)
        gbest = jnp.where(upd, gs[g], gbest)
    v = []
    for j in range(EXPERTS_PER_GROUP):
        vj = r[j]
        for g in range(1, N_GROUPS):
            vj = jnp.where(gidx == g, r[4 * g + j], vj)
        v.append(vj)

    def first_argmax(vals):
        best, idx = vals[0], jnp.zeros_like(gidx)
        for j in range(1, len(vals)):
            upd = vals[j] > best
            idx = jnp.where(upd, j, idx)
            best = jnp.where(upd, vals[j], best)
        return best, idx

    v1, i1 = first_argmax(v)
    v2, i2 = first_argmax([jnp.where(i1 == j, -1.0, v[j]) for j in range(EXPERTS_PER_GROUP)])
    tot = v1 + v2
    w1, w2 = v1 / tot, v2 / tot
    lo = jnp.minimum(i1, i2)
    hi = jnp.maximum(i1, i2)
    pair = jnp.where(lo == 0, hi - 1, jnp.where(lo == 1, hi + 1, 5))
    cls = gidx * 6 + pair
    first_is_lo = i1 < i2
    w_lo = jnp.where(first_is_lo, w1, w2)
    w_hi = jnp.where(first_is_lo, w2, w1)
    return cls, w_lo, w_hi


def _post_kernel(a_ref, w_ref, bias_ref, scale_ref, x_ref, g1_ref, lng_ref, lnb_ref, sc2_ref, sh2_ref,
                 rwt_ref, rb_ref, tri_ref, x1_ref, tokx_ref, ri_ref, cnt_ref, cnt_scr, *, alpha, d):
    first = jnp.logical_and(pl.program_id(0) == 0, pl.program_id(1) == 0)

    @pl.when(first)
    def _():
        cnt_scr[...] = jnp.zeros_like(cnt_scr)

    y = _mm(a_ref[...].astype(BF16), w_ref[...])
    y = (y + bias_ref[...]) * scale_ref[...]
    u = alpha * x_ref[...] + (1.0 + g1_ref[...]) * y
    x1 = _layer_norm(u, lng_ref[...], lnb_ref[...])
    x1_ref[...] = x1
    tok = x1 * (1.0 + sc2_ref[...]) + sh2_ref[...]
    tokx_ref[:, :d] = tok
    tm = tok.shape[0]
    logits_t = _mm_nt(rwt_ref[...], tok.astype(BF16)) + rb_ref[...]
    cls, w_lo, w_hi = _route_rows(logits_t)
    crow = lax.broadcasted_iota(I32, (CLASS_ROWS, tm), 0)
    onehot = (crow == cls).astype(F32)
    prefix = _mm(onehot.astype(BF16), tri_ref[...])
    base = cnt_scr[...]
    rank = jnp.sum(onehot * (prefix + base), axis=0, keepdims=True).astype(I32)
    cnt_new = base + jnp.sum(onehot, axis=1, keepdims=True)
    cnt_scr[...] = cnt_new
    cnt_ref[...] = cnt_new[:, :LANES]
    r8 = lax.broadcasted_iota(I32, (SUBLANES, tm), 0)
    ri_ref[...] = jnp.where(r8 == 0, cls, jnp.where(r8 == 1, rank, 0))
    r128 = lax.broadcasted_iota(I32, (LANES, tm), 0)
    wmat = jnp.where(r128 == 0, w_lo, jnp.where(r128 == 1, w_hi, 0.0))
    tokx_ref[:, d:] = wmat.T


def _post_call(a, w, bias, scale, x, g1, lng, lnb, sc2, sh2, rwt, rb, alpha):
    bsz, l, d = x.shape
    k = a.shape[-1]
    tm = min(ROW_TILE, l)
    nt = l // tm
    t = bsz * l
    tri = jnp.asarray(np.triu(np.ones((tm, tm), np.float32), 1), BF16)
    vec = lambda: pl.BlockSpec((1, d), lambda b, i: (0, 0))
    bvec = lambda: pl.BlockSpec((None, 1, d), lambda b, i: (b, 0, 0))
    return pl.pallas_call(
        functools.partial(_post_kernel, alpha=alpha, d=d),
        out_shape=(
            jax.ShapeDtypeStruct((bsz, l, d), F32),
            jax.ShapeDtypeStruct((t, d + LANES), F32),
            jax.ShapeDtypeStruct((SUBLANES, t), I32),
            jax.ShapeDtypeStruct((CLASS_ROWS, LANES), F32),
        ),
        grid=(bsz, nt),
        in_specs=[
            pl.BlockSpec((None, tm, k), lambda b, i: (b, i, 0)),
            pl.BlockSpec((k, d), lambda b, i: (0, 0)),
            vec(), vec(),
            pl.BlockSpec((None, tm, d), lambda b, i: (b, i, 0)),
            bvec(), vec(), vec(), bvec(), bvec(),
            pl.BlockSpec(rwt.shape, lambda b, i: (0, 0)),
            pl.BlockSpec(rb.shape, lambda b, i: (0, 0)),
            pl.BlockSpec((tm, tm), lambda b, i: (0, 0)),
        ],
        out_specs=(
            pl.BlockSpec((None, tm, d), lambda b, i: (b, i, 0)),
            pl.BlockSpec((tm, d + LANES), lambda b, i: (b * nt + i, 0)),
            pl.BlockSpec((SUBLANES, tm), lambda b, i: (0, b * nt + i)),
            pl.BlockSpec((CLASS_ROWS, LANES), lambda b, i: (0, 0)),
        ),
        scratch_shapes=[pltpu.VMEM((CLASS_ROWS, tm), F32)],
        compiler_params=_cparams(("arbitrary", "arbitrary")),
        name="post",
    )(a, w, bias, scale, x, g1, lng, lnb, sc2, sh2, rwt, rb, tri)


SRC_LEAD = 2 * MOE_BLOCK


def _invperm_kernel(dest_ref, src_ref, *, t, chunk):
    phase = pl.program_id(0)
    base = pl.program_id(1) * chunk

    @pl.when(phase == 0)
    def _():
        def fill(r, c):
            src_ref[base + r] = t + r
            return c
        lax.fori_loop(0, chunk, fill, 0, unroll=8)

    @pl.when(jnp.logical_and(phase == 1, base < t))
    def _():
        def put(i, c):
            src_ref[SRC_LEAD + dest_ref[base + i]] = base + i
            return c
        lax.fori_loop(0, chunk, put, 0, unroll=8)


def _invperm_call(dest, n_rows):
    t = dest.shape[0]
    chunk = 2 * MOE_BLOCK
    assert n_rows % MOE_BLOCK == 0 and t % chunk == 0 and SRC_LEAD == chunk
    n_chunks = -(-(SRC_LEAD + n_rows) // chunk)
    return pl.pallas_call(
        functools.partial(_invperm_kernel, t=t, chunk=chunk),
        out_shape=jax.ShapeDtypeStruct((n_chunks * chunk,), I32),
        grid=(2, n_chunks),
        in_specs=[pl.BlockSpec(memory_space=pltpu.SMEM)],
        out_specs=pl.BlockSpec(memory_space=pltpu.SMEM),
        compiler_params=_cparams(("arbitrary", "arbitrary")),
        name="moe_invperm",
    )(dest)


def _expert_kernel(src_ref, ea_ref, eb_ref, nu_ref, tok_hbm, wga_ref, wua_ref, wda_ref, wgb_ref, wub_ref,
                   wdb_ref, y_hbm, xbuf, ybuf, gsem, ssem, *, d, fc, bm, t):
    del ea_ref, eb_ref
    b = pl.program_id(0)
    nu = nu_ref[0]

    def gather_row(blk, s, r):
        tok = jnp.minimum(src_ref[SRC_LEAD + blk * bm + r], t - 1)
        return pltpu.make_async_copy(tok_hbm.at[pl.ds(tok, 1)], xbuf.at[s, pl.ds(r, 1)], gsem.at[s])

    def scatter_row(blk, s, r):
        return pltpu.make_async_copy(ybuf.at[s, pl.ds(r, 1)],
                                     y_hbm.at[pl.ds(src_ref[SRC_LEAD + blk * bm + r], 1)], ssem.at[s])

    def wait_gather(s):
        pltpu.make_async_copy(tok_hbm.at[pl.ds(0, bm)], xbuf.at[s], gsem.at[s]).wait()

    def wait_scatter(s):
        pltpu.make_async_copy(ybuf.at[s], y_hbm.at[pl.ds(0, bm)], ssem.at[s]).wait()

    @pl.when(b == 0)
    def _():
        ybuf[...] = jnp.zeros_like(ybuf)
        cp = pltpu.make_async_copy(ybuf.at[1], y_hbm.at[pl.ds(t + bm, bm)], ssem.at[1])
        cp.start()
        cp.wait()
        pltpu.make_async_copy(ybuf.at[0], y_hbm.at[pl.ds(t, bm)], ssem.at[0]).start()
        for r in range(bm):
            gather_row(0, 0, r).start()

    def run(slot):
        wait_gather(slot)
        wait_scatter(slot)
        x = xbuf[slot, :, :d].astype(BF16)
        wts = xbuf[slot, :, d:]
        w_lo = wts[:, 0:1]
        w_hi = wts[:, 1:2]
        nxt = jnp.minimum(b + 1, nu - 1)
        for r in range(bm):
            gather_row(nxt, 1 - slot, r).start()
        for r in range(bm):
            scatter_row(b - 1, 1 - slot, r).start()

        def ffn(wg_ref, wu_ref, wd_ref):
            f = wg_ref.shape[1]
            acc = None
            for c in range(f // fc):
                g = _mm(x, wg_ref[:, c * fc:(c + 1) * fc])
                u = _mm(x, wu_ref[:, c * fc:(c + 1) * fc])
                h = (_silu(g) * u).astype(BF16)
                part = _mm(h, wd_ref[c * fc:(c + 1) * fc, :])
                acc = part if acc is None else acc + part
            return acc

        ybuf[slot] = w_lo * ffn(wga_ref, wua_ref, wda_ref) + w_hi * ffn(wgb_ref, wub_ref, wdb_ref)

        @pl.when(b == nu - 1)
        def _():
            for r in range(bm):
                scatter_row(b, slot, r).start()
            wait_gather(1 - slot)
            wait_scatter(1 - slot)
            wait_scatter(slot)

    for s in range(2):
        pl.when(jnp.logical_and(b < nu, b % 2 == s))(functools.partial(run, s))


def _expert_call(src, blk_ea, blk_eb, n_used, tokx, wg, wu, wd):
    t, dx = tokx.shape
    d = dx - LANES
    f = wg.shape[-1]
    bm = MOE_BLOCK
    n_blocks = blk_ea.shape[0]
    amap = lambda b, s, ea, eb, nu: (ea[b], 0, 0)
    bmap = lambda b, s, ea, eb, nu: (eb[b], 0, 0)
    return pl.pallas_call(
        functools.partial(_expert_kernel, d=d, fc=512, bm=bm, t=t),
        out_shape=jax.ShapeDtypeStruct((t + 2 * bm, d), F32),
        grid_spec=pltpu.PrefetchScalarGridSpec(
            num_scalar_prefetch=4,
            grid=(n_blocks,),
            in_specs=[
                pl.BlockSpec(memory_space=pl.ANY),
                pl.BlockSpec((None, d, f), amap), pl.BlockSpec((None, d, f), amap), pl.BlockSpec((None, f, d), amap),
                pl.BlockSpec((None, d, f), bmap), pl.BlockSpec((None, d, f), bmap), pl.BlockSpec((None, f, d), bmap),
            ],
            out_specs=pl.BlockSpec(memory_space=pl.ANY),
            scratch_shapes=[pltpu.VMEM((2, bm, dx), F32), pltpu.VMEM((2, bm, d), F32),
                            pltpu.SemaphoreType.DMA((2,)), pltpu.SemaphoreType.DMA((2,))],
        ),
        compiler_params=_cparams(("arbitrary",)),
        name="experts",
    )(src, blk_ea, blk_eb, n_used, tokx, wg, wu, wd, wg, wu, wd)


def _combine_kernel(y2_ref, x1_ref, g2_ref, lng_ref, lnb_ref, o_ref, *, alpha):
    u = alpha * x1_ref[...] + (1.0 + g2_ref[...]) * y2_ref[...]
    o_ref[...] = _layer_norm(u, lng_ref[...], lnb_ref[...])


def _combine_call(y2, x1, g2, lng, lnb, alpha):
    bsz, l, d = x1.shape
    tm = min(ROW_TILE, l)
    nt = l // tm
    return pl.pallas_call(
        functools.partial(_combine_kernel, alpha=alpha),
        out_shape=jax.ShapeDtypeStruct((bsz, l, d), F32),
        grid=(bsz, nt),
        in_specs=[
            pl.BlockSpec((tm, d), lambda b, i: (b * nt + i, 0)),
            pl.BlockSpec((None, tm, d), lambda b, i: (b, i, 0)),
            pl.BlockSpec((None, 1, d), lambda b, i: (b, 0, 0)),
            pl.BlockSpec((1, d), lambda b, i: (0, 0)),
            pl.BlockSpec((1, d), lambda b, i: (0, 0)),
        ],
        out_specs=pl.BlockSpec((None, tm, d), lambda b, i: (b, i, 0)),
        compiler_params=_cparams(("arbitrary", "arbitrary")),
        name="combine",
    )(y2, x1, g2, lng, lnb)


def _pair_tables():
    ta, tb = [], []
    for g in range(N_GROUPS):
        for a in range(EXPERTS_PER_GROUP):
            for b in range(a + 1, EXPERTS_PER_GROUP):
                ta.append(g * EXPERTS_PER_GROUP + a)
                tb.append(g * EXPERTS_PER_GROUP + b)
    return np.asarray(ta, np.int32), np.asarray(tb, np.int32)


def _moe(x1, tokx, ri, cnt, g2, lng, lnb, wg, wu, wd, alpha):
    bsz, l, d = x1.shape
    t = bsz * l
    bm = MOE_BLOCK
    n_blocks = t // bm + N_PAIR_CLASSES
    counts = cnt[:N_PAIR_CLASSES, 0].astype(I32)
    nblk = (counts + bm - 1) // bm
    cend = jnp.cumsum(nblk)
    cstart = cend - nblk
    dest = cstart[ri[0]] * bm + ri[1]
    blk = jnp.arange(n_blocks, dtype=I32)
    blk_cls = jnp.minimum(jnp.sum((cend[None, :] <= blk[:, None]).astype(I32), axis=1), N_PAIR_CLASSES - 1)
    ta, tb = _pair_tables()
    blk_ea = jnp.asarray(ta)[blk_cls]
    blk_eb = jnp.asarray(tb)[blk_cls]
    n_used = cend[-1:].astype(I32)
    src = _invperm_call(dest, n_blocks * bm)
    y2 = _expert_call(src, blk_ea, blk_eb, n_used, tokx, wg, wu, wd)
    return _combine_call(y2, x1, g2, lng, lnb, alpha)


def _qkv_kernel(x_ref, sc_ref, sh_ref, w_ref, *rest, rope, groups, d, q_scale):
    if rope:
        tab_ref, out_refs = rest[0], rest[1:]
    else:
        tab_ref, out_refs = None, rest
    h = (x_ref[...] * (1.0 + sc_ref[...]) + sh_ref[...]).astype(BF16)
    for (g, transposed), o_ref in zip(groups, out_refs):
        for c in range(d // MXU_DIM):
            col = g * d + c * MXU_DIM
            z = _mm(h, w_ref[:, col:col + MXU_DIM])
            if rope and g < 2:
                heads = []
                for hh in range(MXU_DIM // LANES):
                    zh = z[:, hh * LANES:(hh + 1) * LANES]
                    heads.append(zh * tab_ref[0] + pltpu.roll(zh, LANES - 16, 1) * tab_ref[1]
                                 + pltpu.roll(zh, 16, 1) * tab_ref[2])
                z = jnp.concatenate(heads, axis=1)
            if g == 0:
                z = z * q_scale
            if transposed:
                o_ref[c * MXU_DIM:(c + 1) * MXU_DIM, :] = z.T.astype(BF16)
            else:
                o_ref[:, c * MXU_DIM:(c + 1) * MXU_DIM] = z.astype(BF16)


def _qkv_call(x, sc, sh, w, tab, groups, q_scale):
    bsz, l, d = x.shape
    tm = min(ROW_TILE, l)
    rope = tab is not None
    in_specs = [
        pl.BlockSpec((None, tm, d), lambda b, i: (b, i, 0)),
        pl.BlockSpec((None, 1, d), lambda b, i: (b, 0, 0)),
        pl.BlockSpec((None, 1, d), lambda b, i: (b, 0, 0)),
        pl.BlockSpec(w.shape, lambda b, i: (0, 0)),
    ]
    args = [x, sc, sh, w]
    if rope:
        in_specs.append(pl.BlockSpec((3, tm, LANES), lambda b, i: (0, i, 0)))
        args.append(tab)
    out_shape, out_specs = [], []
    for (_, transposed) in groups:
        if transposed:
            out_shape.append(jax.ShapeDtypeStruct((bsz, d, l), BF16))
            out_specs.append(pl.BlockSpec((None, d, tm), lambda b, i: (b, 0, i)))
        else:
            out_shape.append(jax.ShapeDtypeStruct((bsz, l, d), BF16))
            out_specs.append(pl.BlockSpec((None, tm, d), lambda b, i: (b, i, 0)))
    return pl.pallas_call(
        functools.partial(_qkv_kernel, rope=rope, groups=groups, d=d, q_scale=q_scale),
        out_shape=tuple(out_shape),
        grid=(bsz, l // tm),
        in_specs=in_specs,
        out_specs=tuple(out_specs),
        compiler_params=_cparams(("arbitrary", "arbitrary")),
        name="qkv",
    )(*args)


def _attn_kernel(q_ref, kt_ref, v_ref, kct_ref, vc_ref, lam_ref, subln_ref, o_ref, m_scr, acc_scr, s_scr, p_scr,
                 a_scr, *, tq, tk, lam_init):
    q = q_ref[...]
    hd = q.shape[1]
    lane = lax.broadcasted_iota(I32, q.shape, 1)
    zero = jnp.zeros_like(q)
    lhs = jnp.concatenate([jnp.where(lane < hd // 2, q, zero), jnp.where(lane >= hd // 2, q, zero)], axis=0)
    m_scr[...] = jnp.full_like(m_scr, -jnp.inf)
    acc_scr[...] = jnp.zeros_like(acc_scr)

    def softmax(s):
        m_old = m_scr[...]
        m_new = jnp.maximum(m_old, jnp.max(s, axis=1, keepdims=True))
        m_scr[...] = m_new
        p = jnp.exp2(s - jnp.concatenate([m_new] * (s.shape[1] // LANES), axis=1)).astype(BF16)
        return p, jnp.exp2(m_old - m_new)

    def accumulate(p, a, vb):
        tkk = vb.shape[0]
        ones_col = jnp.where(lax.broadcasted_iota(I32, (tkk, LANES), 1) == 0, 1.0, 0.0).astype(BF16)
        vext = jnp.concatenate([vb, ones_col], axis=1)
        acc_scr[...] = jnp.concatenate([a, a], axis=1) * acc_scr[...] + _mm(p, vext)

    def chunk_start(j):
        return j * tk if isinstance(j, int) else pl.multiple_of(j * tk, tk)

    def k_chunk(j):
        return kt_ref[:, pl.ds(chunk_start(j), tk)]

    def v_chunk(j):
        return v_ref[pl.ds(chunk_start(j), tk), :]

    n_chunks = v_ref.shape[0] // tk
    s_scr[0] = _mm(lhs, k_chunk(0))
    if n_chunks > 1:
        s_scr[1] = _mm(lhs, k_chunk(1))
    p0, a0 = softmax(s_scr[0])
    p_scr[0] = p0
    a_scr[0] = a0

    def stage(j, slot):
        s_next = _mm(lhs, k_chunk(j + 1))
        accumulate(p_scr[1 - slot], a_scr[1 - slot], v_chunk(j - 1))
        p, a = softmax(s_scr[slot])
        s_scr[1 - slot] = s_next
        p_scr[slot] = p
        a_scr[slot] = a

    for j in range(1, n_chunks - 1):
        stage(j, j % 2)
    last = (n_chunks - 1) % 2
    sc = _mm(lhs, kct_ref[...])
    if n_chunks > 1:
        accumulate(p_scr[1 - last], a_scr[1 - last], v_chunk(n_chunks - 2))
        p, a = softmax(s_scr[last])
    else:
        p, a = p0, a0
    accumulate(p, a, v_chunk(n_chunks - 1))
    pc, ac = softmax(sc)
    accumulate(pc, ac, vc_ref[...])

    lam = lam_ref[...]
    lam_full = (jnp.exp(jnp.sum(lam[0:1] * lam[1:2], axis=1, keepdims=True))
                - jnp.exp(jnp.sum(lam[2:3] * lam[3:4], axis=1, keepdims=True)) + lam_init)
    acc = acc_scr[...]
    o12 = acc[:, :hd] / acc[:, hd:hd + 1]
    o = o12[:tq] - lam_full * o12[tq:]
    o = o * lax.rsqrt(jnp.mean(o * o, axis=1, keepdims=True) + LN_EPS)
    o = o * (subln_ref[...] * (1.0 - lam_init))
    o_ref[...] = o.astype(BF16)


def _attn_call(q, kt, v, kct, vc, lam, subln_row, lam_init):
    bsz, l, d = q.shape
    lc = vc.shape[1]
    hd = d // ATT_HEADS
    tq = min(ATT_TQ, l)
    tk = min(ATT_TK, l)
    return pl.pallas_call(
        functools.partial(_attn_kernel, tq=tq, tk=tk, lam_init=lam_init),
        out_shape=jax.ShapeDtypeStruct((bsz, l, d), BF16),
        grid=(bsz, ATT_HEADS, l // tq),
        in_specs=[
            pl.BlockSpec((None, tq, hd), lambda b, h, i: (b, i, h)),
            pl.BlockSpec((None, hd, l), lambda b, h, i: (b, h, 0)),
            pl.BlockSpec((None, l, hd), lambda b, h, i: (b, 0, h)),
            pl.BlockSpec((None, hd, lc), lambda b, h, i: (b, h, 0)),
            pl.BlockSpec((None, lc, hd), lambda b, h, i: (b, 0, h)),
            pl.BlockSpec(lam.shape, lambda b, h, i: (0, 0)),
            pl.BlockSpec(subln_row.shape, lambda b, h, i: (0, 0)),
        ],
        out_specs=pl.BlockSpec((None, tq, hd), lambda b, h, i: (b, i, h)),
        scratch_shapes=[pltpu.VMEM((2 * tq, LANES), F32), pltpu.VMEM((2 * tq, hd + LANES), F32),
                        pltpu.VMEM((2, 2 * tq, tk), F32), pltpu.VMEM((2, 2 * tq, tk), BF16),
                        pltpu.VMEM((2, 2 * tq, LANES), F32)],
        compiler_params=_cparams(("arbitrary", "arbitrary", "arbitrary")),
        name="diff_attn",
    )(q, kt, v, kct, vc, lam, subln_row)


def _rope_tables(l, hd):
    dh = hd // 2
    half = dh // 2
    rows = l // GRID_W
    row = jnp.repeat(jnp.arange(rows), GRID_W).astype(F32)
    col = jnp.tile(jnp.arange(GRID_W), rows).astype(F32)
    inv = ROPE_THETA ** (-jnp.arange(0, half, 2, dtype=F32) / half)
    ar = row[:, None] * inv
    ac = col[:, None] * inv
    ang = jnp.concatenate([ar, ar, ac, ac], axis=-1)
    cos, sin = jnp.cos(ang), jnp.sin(ang)
    first = (np.arange(dh) % half) < half // 2
    sin_a = jnp.where(first, -sin, 0.0)
    sin_b = jnp.where(first, 0.0, sin)
    return jnp.stack([jnp.tile(cos, (1, 2)), jnp.tile(sin_a, (1, 2)), jnp.tile(sin_b, (1, 2))])


def _attention_mixer(x, ctx, sc1, sh1, csc1, csh1, wqkv, lam, subln, layer_idx):
    bsz, l, d = x.shape
    hd = d // ATT_HEADS
    lam_init = 0.8 - 0.6 * math.exp(-0.3 * layer_idx)
    w = wqkv.astype(BF16)
    q, kt, v = _qkv_call(x, sc1, sh1, w, _rope_tables(l, hd), ((0, False), (1, True), (2, False)),
                         (hd // 2) ** -0.5 * math.log2(math.e))
    ones = jnp.ones((bsz, 1, 1), F32)
    kct, vc = _qkv_call(ctx, csc1[None, None, :] * ones, csh1[None, None, :] * ones, w, None,
                        ((1, True), (2, False)), 1.0)
    return _attn_call(q, kt, v, kct, vc, lam, subln.reshape(1, hd), lam_init)


def _halo_specs(tm, d, l):
    per = tm // HALO
    last = l // HALO - 1
    return [
        pl.BlockSpec((None, tm, d), lambda b, i: (b, i, 0)),
        pl.BlockSpec((None, HALO, d), lambda b, i: (b, jnp.maximum(i * per - 1, 0), 0)),
        pl.BlockSpec((None, HALO, d), lambda b, i: (b, jnp.minimum((i + 1) * per, last), 0)),
    ]


def _pool_kernel(x_ref, xp_ref, xn_ref, sc_ref, sh_ref, o_ref, ext, *, tm, l):
    i = pl.program_id(1)
    nt = pl.num_programs(1)
    sc = 1.0 + sc_ref[...]
    sh = sh_ref[...]
    ext[HALO:HALO + tm, :] = x_ref[...] * sc + sh
    ext[0:HALO, :] = jnp.where(i > 0, xp_ref[...] * sc + sh, 0.0)
    ext[HALO + tm:, :] = jnp.where(i < nt - 1, xn_ref[...] * sc + sh, 0.0)
    gw = x_ref.shape[1] // len(POOL_WINDOWS)
    t = i * tm + lax.broadcasted_iota(I32, (tm, gw), 0)
    for g, win in enumerate(POOL_WINDOWS):
        c0 = g * gw
        acc = None
        for j in range(-(win // 2), win - win // 2):
            part = ext[HALO + j:HALO + j + tm, c0:c0 + gw]
            acc = part if acc is None else acc + part
        lo = jnp.maximum(t - win // 2, 0)
        hi = jnp.minimum(t - win // 2 + win, l)
        mean = acc / (hi - lo).astype(F32)
        o_ref[:, c0:c0 + gw] = (mean - ext[HALO:HALO + tm, c0:c0 + gw]).astype(BF16)


def _pool_call(x, sc1, sh1):
    bsz, l, d = x.shape
    tm = min(ROW_TILE, l)
    bvec = lambda: pl.BlockSpec((None, 1, d), lambda b, i: (b, 0, 0))
    return pl.pallas_call(
        functools.partial(_pool_kernel, tm=tm, l=l),
        out_shape=jax.ShapeDtypeStruct((bsz, l, d), BF16),
        grid=(bsz, l // tm),
        in_specs=_halo_specs(tm, d, l) + [bvec(), bvec()],
        out_specs=pl.BlockSpec((None, tm, d), lambda b, i: (b, i, 0)),
        scratch_shapes=[pltpu.VMEM((tm + 2 * HALO, d), F32)],
        compiler_params=_cparams(("arbitrary", "arbitrary")),
        name="pool",
    )(x, x, x, sc1, sh1)


KRON = SUBLANES
INNER = 128
N_RB = INNER // KRON


def _trig(num, den):
    ang = jnp.asarray(np.mod(num, den), F32) * (2.0 * math.pi / den)
    return jnp.cos(ang), jnp.sin(ang)


def _kron_rows(m):
    p, q = m.shape
    rep_r = jnp.asarray(np.kron(np.eye(p), np.ones((KRON, 1))), F32)
    rep_c = jnp.asarray(np.kron(np.eye(q), np.ones((1, KRON))), F32)
    hi = lax.Precision.HIGHEST
    big = jnp.dot(jnp.dot(rep_r, m, precision=hi), rep_c, precision=hi)
    r = lax.broadcasted_iota(I32, big.shape, 0) % KRON
    c = lax.broadcasted_iota(I32, big.shape, 1) % KRON
    return jnp.where(r == c, big, 0.0)


def _outer_kernel(*refs, n_in):
    m_ref, x_refs, o_ref = refs[0], refs[1:1 + n_in], refs[1 + n_in]
    acc = None
    col = 0
    for x_ref in x_refs:
        a, j, d = x_ref.shape
        x = x_ref[...].reshape(a * j, d).astype(BF16)
        part = _mm(m_ref[:, col:col + a * j], x)
        acc = part if acc is None else acc + part
        col += a * j
    o_ref[...] = acc.astype(o_ref.dtype)


def _outer_fwd_call(m, xs, a_out):
    bsz, rows, d = xs[0].shape
    a_in = rows // INNER
    xs4 = [x.reshape(bsz, a_in, N_RB, KRON, d) for x in xs]
    return pl.pallas_call(
        functools.partial(_outer_kernel, n_in=len(xs)),
        out_shape=jax.ShapeDtypeStruct((bsz, N_RB, a_out * 2 * KRON, d), BF16),
        grid=(bsz, N_RB),
        in_specs=[pl.BlockSpec(m.shape, lambda b, r: (0, 0))]
        + [pl.BlockSpec((None, a_in, None, KRON, d), lambda b, r: (b, 0, r, 0, 0)) for _ in xs],
        out_specs=pl.BlockSpec((None, None, a_out * 2 * KRON, d), lambda b, r: (b, r, 0, 0)),
        compiler_params=_cparams(("arbitrary", "arbitrary")),
        name="dft_outer",
    )(m, *xs4)


def _fn_chan_kernel(x_ref, sc_ref, sh_ref, cs_ref, pr_ref, pi_ref, *, gw):
    h = (x_ref[...] * (1.0 + sc_ref[...]) + sh_ref[...]).astype(BF16)
    for g in range(h.shape[1] // gw):
        z = _mm(h[:, g * gw:(g + 1) * gw], cs_ref[...])
        pr_ref[:, g * gw:(g + 1) * gw] = z[:, :gw]
        pi_ref[:, g * gw:(g + 1) * gw] = z[:, gw:]


def _fn_chan_call(x, sc1, sh1):
    bsz, l, d = x.shape
    gw = d // FFT_GROUPS
    tm = min(ROW_TILE, l)
    cc, ss = _trig(np.outer(np.arange(gw), np.arange(gw)), gw)
    cs = jnp.concatenate([cc, -ss], axis=1).astype(BF16)
    bvec = lambda: pl.BlockSpec((None, 1, d), lambda b, i: (b, 0, 0))
    tile = lambda: pl.BlockSpec((None, tm, d), lambda b, i: (b, i, 0))
    return pl.pallas_call(
        functools.partial(_fn_chan_kernel, gw=gw),
        out_shape=(jax.ShapeDtypeStruct((bsz, l, d), F32), jax.ShapeDtypeStruct((bsz, l, d), F32)),
        grid=(bsz, l // tm),
        in_specs=[tile(), bvec(), bvec(), pl.BlockSpec(cs.shape, lambda b, i: (0, 0))],
        out_specs=(tile(), tile()),
        compiler_params=_cparams(("arbitrary", "arbitrary")),
        name="fnet_channel_dft",
    )(x, sc1, sh1, cs)


def _fn_inner_kernel(m_ref, x_ref, o_ref):
    rb, j1, cj, d = x_ref.shape
    x = x_ref[...].reshape(rb * j1 * cj, d)
    y = _mm(m_ref[...], x)
    o_ref[...] = y.reshape(o_ref.shape)


def _fn_inner_call(m2, spec, a):
    bsz, _, _, d = spec.shape
    spec5 = spec.reshape(bsz, N_RB, a, 2 * KRON, d)
    out = pl.pallas_call(
        _fn_inner_kernel,
        out_shape=jax.ShapeDtypeStruct((bsz, INNER, a, d), F32),
        grid=(a // KRON, bsz),
        in_specs=[
            pl.BlockSpec((None,) + m2.shape[1:], lambda s, b: (s, 0, 0)),
            pl.BlockSpec((None, N_RB, KRON, 2 * KRON, d), lambda s, b: (b, 0, s, 0, 0)),
        ],
        out_specs=pl.BlockSpec((None, INNER, KRON, d), lambda s, b: (b, 0, s, 0)),
        compiler_params=_cparams(("arbitrary", "arbitrary")),
        name="fnet_inner_dft",
    )(m2, spec5)
    return out.reshape(bsz, INNER * a, d)


def _fnet_tables(l, gw):
    a = l // INNER
    k1 = np.arange(a)
    c1, s1 = _trig(np.outer(k1, k1), a)
    m1 = jnp.stack([jnp.stack([c1, s1], axis=1), jnp.stack([-s1, c1], axis=1)], axis=1)
    m1k = _kron_rows(m1.reshape(2 * a, 2 * a))
    rb, j1c, cc, jj = np.meshgrid(np.arange(N_RB), np.arange(KRON), np.arange(2), np.arange(KRON), indexing="ij")
    l2 = (rb * KRON + jj).reshape(-1)
    j1c, cc = j1c.reshape(-1), cc.reshape(-1)
    norm = 1.0 / math.sqrt(l * gw)
    ca, sa = _trig(np.outer(np.arange(INNER), l2), INNER)
    p_tab = jnp.where(cc == 0, ca, sa) * norm
    q_tab = jnp.where(cc == 0, -sa, ca) * norm
    k1_col = KRON * np.arange(a // KRON)[:, None] + j1c[None, :]
    cb, sb = _trig(k1_col * l2[None, :], l)
    t3 = p_tab[None] * cb[:, None, :] + q_tab[None] * sb[:, None, :]
    mask = jnp.asarray(j1c[None, :] == np.arange(KRON)[:, None], F32)
    m2 = (t3[:, :, None, :] * mask[None, None]).reshape(a // KRON, INNER * KRON, l2.shape[0])
    return m1k.astype(BF16), m2.astype(BF16)


def _fourier_mixer(x, sc1, sh1):
    bsz, l, d = x.shape
    a = l // INNER
    m1k, m2 = _fnet_tables(l, d // FFT_GROUPS)
    pr, pi = _fn_chan_call(x, sc1, sh1)
    spec = _outer_fwd_call(m1k, [pr, pi], a)
    return _fn_inner_call(m2, spec, a)


def _hy_front_kernel(x_ref, xp_ref, xn_ref, sc_ref, sh_ref, w_ref, bin_ref, cw_ref, cb_ref,
                     vg_ref, x0_ref, hext, zscr, *, tm, l, d):
    i = pl.program_id(1)
    sc = 1.0 + sc_ref[...]
    sh = sh_ref[...]
    hext[HALO:HALO + tm, :] = (x_ref[...] * sc + sh).astype(BF16)
    hext[0:HALO, :] = (xp_ref[...] * sc + sh).astype(BF16)
    hext[HALO + tm:, :] = (xn_ref[...] * sc + sh).astype(BF16)
    pos = i * tm - HALO + lax.broadcasted_iota(I32, (tm + 2 * HALO, MXU_DIM), 0)

    def conv(col):
        z = _mm(hext[...], w_ref[:, col:col + MXU_DIM]) + bin_ref[:, col:col + MXU_DIM]
        zscr[...] = jnp.where(pos >= 0, jnp.where(pos < l, z, 0.0), 0.0)
        cw = cw_ref[:, col:col + MXU_DIM]
        return (zscr[HALO - 1:HALO - 1 + tm, :] * cw[0:1] + zscr[HALO:HALO + tm, :] * cw[1:2]
                + zscr[HALO + 1:HALO + 1 + tm, :] * cw[2:3] + cb_ref[:, col:col + MXU_DIM])

    for c in range(d // MXU_DIM):
        cols = slice(c * MXU_DIM, (c + 1) * MXU_DIM)
        x0_ref[:, cols] = conv(c * MXU_DIM)
        x1 = conv(d + c * MXU_DIM)
        v = conv(2 * d + c * MXU_DIM)
        vg_ref[:, cols] = v * x1


def _hy_front_call(x, sc1, sh1, win, bin_, conv_w, conv_b):
    bsz, l, d = x.shape
    tm = min(ROW_TILE, l)
    n3 = win.shape[1]
    bvec = lambda: pl.BlockSpec((None, 1, d), lambda b, i: (b, 0, 0))
    full = lambda a: pl.BlockSpec(a.shape, lambda b, i: (0, 0))
    tile = lambda: pl.BlockSpec((None, tm, d), lambda b, i: (b, i, 0))
    args = (win.astype(BF16), bin_.reshape(1, n3), conv_w, conv_b.reshape(1, n3))
    return pl.pallas_call(
        functools.partial(_hy_front_kernel, tm=tm, l=l, d=d),
        out_shape=(jax.ShapeDtypeStruct((bsz, l, d), F32), jax.ShapeDtypeStruct((bsz, l, d), F32)),
        grid=(bsz, l // tm),
        in_specs=_halo_specs(tm, d, l) + [bvec(), bvec()] + [full(a) for a in args],
        out_specs=(tile(), tile()),
        scratch_shapes=[pltpu.VMEM((tm + 2 * HALO, d), BF16), pltpu.VMEM((tm + 2 * HALO, MXU_DIM), F32)],
        compiler_params=_cparams(("arbitrary", "arbitrary")),
        name="hyena_front",
    )(x, x, x, sc1, sh1, *args)


def _hy_filter_kernel(z_ref, f1_ref, fb1_ref, f2_ref, fb2_ref, f3_ref, fr_ref, del_ref, o_ref, *, tf, l):
    z = z_ref[...]
    fr = fr_ref[...]
    a = jnp.sin(fr * (_mm(z, f1_ref[...]) + fb1_ref[...]))
    a = jnp.sin(fr * (_mm(a, f2_ref[...]) + fb2_ref[...]))
    k = _mm(a, f3_ref[...])
    window = jnp.exp(-z[:, 0:1] * del_ref[...])
    row = pl.program_id(0) * tf + lax.broadcasted_iota(I32, k.shape, 0)
    o_ref[...] = jnp.where(row == l, 0.0, k * window)


def _hy_filter_call(l, d, f1, fb1, f2, fb2, f3, freq):
    n = 2 * l
    tf = min(ROW_TILE, l)
    hid = f1.shape[1]
    pos = jnp.concatenate([jnp.arange(l, dtype=F32), jnp.zeros((1,), F32),
                           jnp.arange(l - 1, 0, -1, dtype=F32)])
    bands = jnp.linspace(1e-4, HY_BANDS - 1, HY_BANDS, dtype=F32)
    ang = (2.0 * math.pi / l) * pos[:, None] * bands[None, :]
    feat = jnp.concatenate([(pos / l)[:, None], jnp.cos(ang), jnp.sin(ang)], axis=-1)
    kin = 64
    feat = jnp.pad(feat, ((0, 0), (0, kin - feat.shape[1])))
    f1p = jnp.pad(f1, ((0, kin - f1.shape[0]), (0, 0)))
    deltas = jnp.abs(jnp.linspace(math.log(HY_TARGET) / HY_SLOW, math.log(HY_TARGET) / HY_FAST, d,
                                  dtype=F32)).reshape(1, d)
    half = l // tf
    full = lambda a: pl.BlockSpec(a.shape, lambda i: (0, 0))
    args = (f1p, fb1.reshape(1, hid), f2, fb2.reshape(1, hid))
    tail = (freq.reshape(1, hid), jnp.asarray(deltas))
    return pl.pallas_call(
        functools.partial(_hy_filter_kernel, tf=tf, l=l),
        out_shape=jax.ShapeDtypeStruct((n, d), F32),
        grid=(n // tf,),
        in_specs=[pl.BlockSpec((tf, kin), lambda i: (i, 0))] + [full(a) for a in args]
        + [pl.BlockSpec((hid, d), lambda i: (0, i // half))] + [full(a) for a in tail],
        out_specs=pl.BlockSpec((tf, d), lambda i: (i, 0)),
        compiler_params=_cparams(("arbitrary",)),
        name="hyena_filter",
    )(feat, *args, f3, *tail)


def _hy_kspec_kernel(mf_ref, x_ref, o_ref):
    rb, cj, d = x_ref.shape
    o_ref[...] = _mm(mf_ref[...], x_ref[...].reshape(rb * cj, d))


def _hy_kspec_call(mf2, spec, a):
    _, _, _, d = spec.shape
    spec5 = spec.reshape(1, N_RB, a, 2 * KRON, d)
    return pl.pallas_call(
        _hy_kspec_kernel,
        out_shape=jax.ShapeDtypeStruct((a, 2 * INNER, d), F32),
        grid=(a,),
        in_specs=[
            pl.BlockSpec((None, 2 * INNER, 2 * INNER), lambda k: (k, 0, 0)),
            pl.BlockSpec((None, N_RB, None, 2 * KRON, d), lambda k: (0, 0, k, 0, 0)),
        ],
        out_specs=pl.BlockSpec((None, 2 * INNER, d), lambda k: (k, 0, 0)),
        compiler_params=_cparams(("arbitrary",)),
        name="hyena_filter_spectrum",
    )(mf2, spec5)


def _hy_mid_kernel(mf_ref, mi_ref, k_ref, x_ref, o_ref):
    bsz, rb, cj, d = x_ref.shape
    kk = k_ref[...]
    kr, ki = kk[:INNER], kk[INNER:]
    for b in range(bsz):
        v = _mm(mf_ref[...], x_ref[b].reshape(rb * cj, d))
        vr, vi = v[:INNER], v[INNER:]
        z = jnp.concatenate([vr * kr - vi * ki, vr * ki + vi * kr], axis=0).astype(BF16)
        y = _mm(mi_ref[...], z)
        o_ref[b] = y.astype(BF16).reshape(rb, cj, d)


def _hy_mid_call(mf2, mi1, kspec, spec, a):
    bsz, _, _, d = spec.shape
    spec5 = spec.reshape(bsz, N_RB, a, 2 * KRON, d)
    mat = lambda: pl.BlockSpec((None, 2 * INNER, 2 * INNER), lambda k: (k, 0, 0))
    blk = lambda: pl.BlockSpec((bsz, N_RB, None, 2 * KRON, d), lambda k: (0, 0, k, 0, 0))
    out = pl.pallas_call(
        _hy_mid_kernel,
        out_shape=jax.ShapeDtypeStruct(spec5.shape, BF16),
        grid=(a,),
        in_specs=[mat(), mat(), pl.BlockSpec((None, 2 * INNER, d), lambda k: (k, 0, 0)), blk()],
        out_specs=blk(),
        compiler_params=_cparams(("arbitrary",)),
        name="hyena_spectral_product",
    )(mf2, mi1, kspec, spec5)
    return out.reshape(spec.shape)


def _hy_out_kernel(m_ref, b_ref, vg_ref, x0_ref, skip_ref, o_ref):
    a2, j, d = vg_ref.shape
    y = _mm(m_ref[...], b_ref[...])
    vg = vg_ref[...].reshape(a2 * j, d)
    x0 = x0_ref[...].reshape(a2 * j, d)
    o_ref[...] = ((y + vg * skip_ref[...]) * x0).reshape(a2, j, d)


def _hy_out_call(mi2, spec, vg, x0, skip):
    bsz, l, d = vg.shape
    a2 = l // INNER
    view = lambda t: t.reshape(bsz, a2, N_RB, KRON, d)
    blk = lambda: pl.BlockSpec((None, a2, None, KRON, d), lambda b, r: (b, 0, r, 0, 0))
    out = pl.pallas_call(
        _hy_out_kernel,
        out_shape=jax.ShapeDtypeStruct((bsz, a2, N_RB, KRON, d), F32),
        grid=(bsz, N_RB),
        in_specs=[
            pl.BlockSpec(mi2.shape, lambda b, r: (0, 0)),
            pl.BlockSpec((None, None) + spec.shape[2:], lambda b, r: (b, r, 0, 0)),
            blk(), blk(),
            pl.BlockSpec((1, d), lambda b, r: (0, 0)),
        ],
        out_specs=blk(),
        compiler_params=_cparams(("arbitrary", "arbitrary")),
        name="hyena_inverse_outer",
    )(mi2, spec, view(vg), view(x0), skip.reshape(1, d))
    return out.reshape(bsz, l, d)


def _hyena_tables(l):
    n = 2 * l
    a = n // INNER
    kb = np.arange(a)
    cf, sf = _trig(np.outer(kb, kb), a)
    fwd = jnp.stack([cf, -sf], axis=1)
    mf1 = _kron_rows(fwd[:, :, :a // 2].reshape(2 * a, a // 2))
    mk1 = _kron_rows(fwd.reshape(2 * a, a))
    inv = jnp.stack([cf, -sf], axis=2)[:a // 2] * (1.0 / n)
    mi2 = _kron_rows(inv.reshape(a // 2, 2 * a))
    rb, cc, jj = np.meshgrid(np.arange(N_RB), np.arange(2), np.arange(KRON), indexing="ij")
    li = (rb * KRON + jj).reshape(-1)
    cc = cc.reshape(-1)
    ca, sa = _trig(np.outer(np.arange(INNER), li), INNER)
    p_tab = jnp.concatenate([jnp.where(cc == 0, ca, sa), jnp.where(cc == 0, -sa, ca)], axis=0)
    q_tab = jnp.concatenate([jnp.where(cc == 0, -sa, ca), jnp.where(cc == 0, -ca, -sa)], axis=0)
    cb, sb = _trig(np.outer(kb, li), n)
    mf2 = (p_tab[None] * cb[:, None, :] + q_tab[None] * sb[:, None, :]).astype(BF16)
    mi1 = (p_tab.T[None] * cb[:, :, None] + q_tab.T[None] * sb[:, :, None]).astype(BF16)
    return (mf1.astype(BF16), mk1.astype(BF16), mi2.astype(BF16), mf2, mi1)


def _hyena_mixer(x, sc1, sh1, win, bin_, conv_w, conv_b, f1, fb1, f2, fb2, f3, freq, skip):
    bsz, l, d = x.shape
    a = 2 * l // INNER
    mf1, mk1, mi2, mf2, mi1 = _hyena_tables(l)
    vg, x0 = _hy_front_call(x, sc1, sh1, win, bin_, conv_w, conv_b)
    kfull = _hy_filter_call(l, d, f1, fb1, f2, fb2, f3, freq)
    kspec = _hy_kspec_call(mf2, _outer_fwd_call(mk1, [kfull[None]], a), a)
    spec = _outer_fwd_call(mf1, [vg], a)
    spec = _hy_mid_call(mf2, mi1, kspec, spec, a)
    return _hy_out_call(mi2, spec, vg, x0, skip)


def kernel(x, c, ctx, c_ctx, mod_w, mod_b, ln_g, ln_b, attn_wqkv, attn_wo, attn_lam, attn_subln, pool_w,
           pool_scale, fnet_w, fnet_b, hy_win, hy_bin, hy_conv_w, hy_conv_b, hy_f1, hy_fb1, hy_f2, hy_fb2,
           hy_f3, hy_freq, hy_skip, hy_wo, hy_bo, router_w, router_b, moe_wg, moe_wu, moe_wd):
    bsz, l, d = x.shape
    depth = mod_w.shape[0]
    n_mixers = 4
    assert depth <= n_mixers
    assert bsz + 1 <= SUBLANES
    alpha = (2 * depth) ** 0.25
    c8 = jnp.zeros((SUBLANES, d), F32).at[:bsz].set(c).at[bsz].set(c_ctx)
    mod = _mod_call(c8, mod_w, mod_b)
    rwt = router_w.T.astype(BF16)
    rb = router_b.reshape(-1, 1)
    zeros = jnp.zeros((1, d), F32)
    ones = jnp.ones((1, d), F32)
    xl = x
    for i in range(depth):
        kind, j = i % n_mixers, i // n_mixers
        ml = mod[i, :bsz]
        sh1, sc1, g1, sh2, sc2, g2 = [ml[:, k * d:(k + 1) * d][:, None, :] for k in range(6)]
        bias, scale = zeros, ones
        if kind == 0:
            mc = mod[i, bsz]
            a = _attention_mixer(xl, ctx, sc1, sh1, mc[d:2 * d], mc[:d], attn_wqkv[j], attn_lam[j],
                                 attn_subln[j], i)
            w = attn_wo[j]
        elif kind == 1:
            a = _pool_call(xl, sc1, sh1)
            w = jax.scipy.linalg.block_diag(*[pool_w[j, g] for g in range(pool_w.shape[1])])
            scale = pool_scale[j].reshape(1, d)
        elif kind == 2:
            a = _fourier_mixer(xl, sc1, sh1)
            w, bias = fnet_w[j], fnet_b[j].reshape(1, d)
        else:
            a = _hyena_mixer(xl, sc1, sh1, hy_win[j], hy_bin[j], hy_conv_w[j], hy_conv_b[j], hy_f1[j],
                             hy_fb1[j], hy_f2[j], hy_fb2[j], hy_f3[j], hy_freq[j], hy_skip[j])
            w, bias = hy_wo[j], hy_bo[j].reshape(1, d)
        x1, tokx, ri, cnt = _post_call(a, w.astype(BF16), bias, scale, xl, g1, ln_g[i, 0:1], ln_b[i, 0:1],
                                       sc2, sh2, rwt, rb, alpha)
        xl = _moe(x1, tokx, ri, cnt, g2, ln_g[i, 1:2], ln_b[i, 1:2], moe_wg[i].astype(BF16),
                  moe_wu[i].astype(BF16), moe_wd[i].astype(BF16), alpha)
    return xl
```

```python
import functools
import math

import numpy as np
import jax
import jax.numpy as jnp
from jax import lax
from jax.experimental import pallas as pl
from jax.experimental.pallas import tpu as pltpu

F32 = jnp.float32
BF16 = jnp.bfloat16
I32 = jnp.int32

GRID_W = 64
ATT_HEADS = 8
ROPE_THETA = 10000.0
POOL_WINDOWS = (2, 4, 8, 16)
FFT_GROUPS = 4
HY_BANDS = 16
HY_TARGET = 1e-2
HY_FAST = 0.3
HY_SLOW = 1.5
N_GROUPS = 4
EXPERTS_PER_GROUP = 4
LN_EPS = 1e-5

LANES = 128
SUBLANES = 8
MXU_DIM = 256
VMEM_LIMIT = 56 * 1024 * 1024

ROW_TILE = 512
MOE_BLOCK = 256
N_PAIR_CLASSES = 24
CLASS_ROWS = 32
ATT_TQ = 256
ATT_TK = 512
HALO = 16


def _cparams(sem):
    return pltpu.CompilerParams(dimension_semantics=sem, vmem_limit_bytes=VMEM_LIMIT)


def _mm(a, b):
    return jnp.dot(a, b, preferred_element_type=F32)


def _mm_nt(a, b):
    return lax.dot_general(a, b, (((1,), (1,)), ((), ())), preferred_element_type=F32)


def _silu(x):
    return x * jax.nn.sigmoid(x)


def _layer_norm(u, g, b):
    mu = jnp.mean(u, axis=-1, keepdims=True)
    d = u - mu
    var = jnp.mean(d * d, axis=-1, keepdims=True)
    return d * lax.rsqrt(var + LN_EPS) * g + b


def _mod_kernel(c_ref, w_ref, b_ref, o_ref):
    s = _silu(c_ref[...]).astype(BF16)
    o_ref[0] = _mm(s, w_ref[0].astype(BF16)) + b_ref[0]


def _mod_call(c8, mod_w, mod_b):
    depth, d, n = mod_w.shape
    tn = 1536
    return pl.pallas_call(
        _mod_kernel,
        out_shape=jax.ShapeDtypeStruct((depth, 8, n), F32),
        grid=(depth, n // tn),
        in_specs=[
            pl.BlockSpec((8, d), lambda i, j: (0, 0)),
            pl.BlockSpec((1, d, tn), lambda i, j: (i, 0, j)),
            pl.BlockSpec((1, 1, tn), lambda i, j: (i, 0, j)),
        ],
        out_specs=pl.BlockSpec((1, 8, tn), lambda i, j: (i, 0, j)),
        compiler_params=_cparams(("arbitrary", "arbitrary")),
        name="mod",
    )(c8, mod_w, mod_b.reshape(depth, 1, n))


def _route_rows(logits_t):
    m = jnp.max(logits_t, axis=0, keepdims=True)
    e = jnp.exp(logits_t - m)
    p = e / jnp.sum(e, axis=0, keepdims=True)
    r = [p[i:i + 1, :] for i in range(N_GROUPS * EXPERTS_PER_GROUP)]
    gs = []
    for g in range(N_GROUPS):
        q = r[4 * g:4 * g + 4]
        best = q[0] + q[1]
        for (i, j) in ((0, 2), (0, 3), (1, 2), (1, 3), (2, 3)):
            best = jnp.maximum(best, q[i] + q[j])
        gs.append(best)
    gbest, gidx = gs[0], jnp.zeros_like(gs[0], dtype=I32)
    for g in range(1, N_GROUPS):
        upd = gs[g] > gbest
        gidx = jnp.where(upd, g, gidx)
        gbest = jnp.where(upd, gs[g], gbest)
    v = []
    for j in range(EXPERTS_PER_GROUP):
        vj = r[j]
        for g in range(1, N_GROUPS):
            vj = jnp.where(gidx == g, r[4 * g + j], vj)
        v.append(vj)

    def first_argmax(vals):
        best, idx = vals[0], jnp.zeros_like(gidx)
        for j in range(1, len(vals)):
            upd = vals[j] > best
            idx = jnp.where(upd, j, idx)
            best = jnp.where(upd, vals[j], best)
        return best, idx

    v1, i1 = first_argmax(v)
    v2, i2 = first_argmax([jnp.where(i1 == j, -1.0, v[j]) for j in range(EXPERTS_PER_GROUP)])
    tot = v1 + v2
    w1, w2 = v1 / tot, v2 / tot
    lo = jnp.minimum(i1, i2)
    hi = jnp.maximum(i1, i2)
    pair = jnp.where(lo == 0, hi - 1, jnp.where(lo == 1, hi + 1, 5))
    cls = gidx * 6 + pair
    first_is_lo = i1 < i2
    w_lo = jnp.where(first_is_lo, w1, w2)
    w_hi = jnp.where(first_is_lo, w2, w1)
    return cls, w_lo, w_hi


def _post_kernel(a_ref, w_ref, bias_ref, scale_ref, x_ref, g1_ref, lng_ref, lnb_ref, sc2_ref, sh2_ref,
                 rwt_ref, rb_ref, tri_ref, x1_ref, tokx_ref, ri_ref, cnt_ref, cnt_scr, *, alpha, d):
    first = jnp.logical_and(pl.program_id(0) == 0, pl.program_id(1) == 0)

    @pl.when(first)
    def _():
        cnt_scr[...] = jnp.zeros_like(cnt_scr)

    y = _mm(a_ref[...].astype(BF16), w_ref[...])
    y = (y + bias_ref[...]) * scale_ref[...]
    u = alpha * x_ref[...] + (1.0 + g1_ref[...]) * y
    x1 = _layer_norm(u, lng_ref[...], lnb_ref[...])
    x1_ref[...] = x1
    tok = x1 * (1.0 + sc2_ref[...]) + sh2_ref[...]
    tokx_ref[:, :d] = tok
    tm = tok.shape[0]
    logits_t = _mm_nt(rwt_ref[...], tok.astype(BF16)) + rb_ref[...]
    cls, w_lo, w_hi = _route_rows(logits_t)
    crow = lax.broadcasted_iota(I32, (CLASS_ROWS, tm), 0)
    onehot = (crow == cls).astype(F32)
    prefix = _mm(onehot.astype(BF16), tri_ref[...])
    base = cnt_scr[...]
    rank = jnp.sum(onehot * (prefix + base), axis=0, keepdims=True).astype(I32)
    cnt_new = base + jnp.sum(onehot, axis=1, keepdims=True)
    cnt_scr[...] = cnt_new
    cnt_ref[...] = cnt_new[:, :LANES]
    r8 = lax.broadcasted_iota(I32, (SUBLANES, tm), 0)
    ri_ref[...] = jnp.where(r8 == 0, cls, jnp.where(r8 == 1, rank, 0))
    r128 = lax.broadcasted_iota(I32, (LANES, tm), 0)
    wmat = jnp.where(r128 == 0, w_lo, jnp.where(r128 == 1, w_hi, 0.0))
    tokx_ref[:, d:] = wmat.T


def _post_call(a, w, bias, scale, x, g1, lng, lnb, sc2, sh2, rwt, rb, alpha):
    bsz, l, d = x.shape
    k = a.shape[-1]
    tm = min(ROW_TILE, l)
    nt = l // tm
    t = bsz * l
    tri = jnp.asarray(np.triu(np.ones((tm, tm), np.float32), 1), BF16)
    vec = lambda: pl.BlockSpec((1, d), lambda b, i: (0, 0))
    bvec = lambda: pl.BlockSpec((None, 1, d), lambda b, i: (b, 0, 0))
    return pl.pallas_call(
        functools.partial(_post_kernel, alpha=alpha, d=d),
        out_shape=(
            jax.ShapeDtypeStruct((bsz, l, d), F32),
            jax.ShapeDtypeStruct((t, d + LANES), F32),
            jax.ShapeDtypeStruct((SUBLANES, t), I32),
            jax.ShapeDtypeStruct((CLASS_ROWS, LANES), F32),
        ),
        grid=(bsz, nt),
        in_specs=[
            pl.BlockSpec((None, tm, k), lambda b, i: (b, i, 0)),
            pl.BlockSpec((k, d), lambda b, i: (0, 0)),
            vec(), vec(),
            pl.BlockSpec((None, tm, d), lambda b, i: (b, i, 0)),
            bvec(), vec(), vec(), bvec(), bvec(),
            pl.BlockSpec(rwt.shape, lambda b, i: (0, 0)),
            pl.BlockSpec(rb.shape, lambda b, i: (0, 0)),
            pl.BlockSpec((tm, tm), lambda b, i: (0, 0)),
        ],
        out_specs=(
            pl.BlockSpec((None, tm, d), lambda b, i: (b, i, 0)),
            pl.BlockSpec((tm, d + LANES), lambda b, i: (b * nt + i, 0)),
            pl.BlockSpec((SUBLANES, tm), lambda b, i: (0, b * nt + i)),
            pl.BlockSpec((CLASS_ROWS, LANES), lambda b, i: (0, 0)),
        ),
        scratch_shapes=[pltpu.VMEM((CLASS_ROWS, tm), F32)],
        compiler_params=_cparams(("arbitrary", "arbitrary")),
        name="post",
    )(a, w, bias, scale, x, g1, lng, lnb, sc2, sh2, rwt, rb, tri)


def _invperm_kernel(dest_ref, pad_lo_ref, pad_hi_ref, src_ref, *, t, chunk, n_spans):
    step = pl.program_id(0)

    @pl.when(step == 0)
    def _():
        def fill(r, c):
            src_ref[r] = t + (r & (chunk - 1))
            return c
        for span in range(n_spans):
            lax.fori_loop(pad_lo_ref[span], pad_hi_ref[span], fill, 0)

    @pl.when(step > 0)
    def _():
        base = (step - 1) * chunk

        def put(i, c):
            src_ref[dest_ref[base + i]] = base + i
            return c
        lax.fori_loop(0, chunk, put, 0, unroll=8)


def _invperm_call(dest, pad_lo, pad_hi, n_rows):
    t = dest.shape[0]
    chunk = 2 * MOE_BLOCK
    assert t % chunk == 0 and chunk & (chunk - 1) == 0
    smem = lambda: pl.BlockSpec(memory_space=pltpu.SMEM)
    return pl.pallas_call(
        functools.partial(_invperm_kernel, t=t, chunk=chunk, n_spans=pad_lo.shape[0]),
        out_shape=jax.ShapeDtypeStruct((n_rows,), I32),
        grid=(1 + t // chunk,),
        in_specs=[smem(), smem(), smem()],
        out_specs=smem(),
        compiler_params=_cparams(("arbitrary",)),
        name="moe_invperm",
    )(dest, pad_lo, pad_hi)


def _expert_kernel(src_ref, ea_ref, eb_ref, nu_ref, tok_hbm, wga_ref, wua_ref, wda_ref, wgb_ref, wub_ref,
                   wdb_ref, y_hbm, xbuf, ybuf, gsem, ssem, *, d, fc, bm, t):
    del ea_ref, eb_ref
    b = pl.program_id(0)
    nu = nu_ref[0]
    slot = b % 2

    def gather_row(blk, s, r):
        tok = jnp.minimum(src_ref[blk * bm + r], t - 1)
        return pltpu.make_async_copy(tok_hbm.at[pl.ds(tok, 1)], xbuf.at[s, pl.ds(r, 1)], gsem.at[s])

    def scatter_row(blk, s, r):
        return pltpu.make_async_copy(ybuf.at[s, pl.ds(r, 1)], y_hbm.at[pl.ds(src_ref[blk * bm + r], 1)],
                                     ssem.at[s])

    def wait_gather(s):
        pltpu.make_async_copy(tok_hbm.at[pl.ds(0, bm)], xbuf.at[s], gsem.at[s]).wait()

    def wait_scatter(s):
        pltpu.make_async_copy(ybuf.at[s], y_hbm.at[pl.ds(0, bm)], ssem.at[s]).wait()

    @pl.when(b == 0)
    def _():
        ybuf[...] = jnp.zeros_like(ybuf)
        for s in range(2):
            cp = pltpu.make_async_copy(ybuf.at[s], y_hbm.at[pl.ds(t + s * bm, bm)], ssem.at[s])
            cp.start()
            cp.wait()
        for r in range(bm):
            gather_row(0, 0, r).start()

    @pl.when(b < nu)
    def _():
        wait_gather(slot)

        @pl.when(b >= 2)
        def _():
            wait_scatter(slot)

        nxt = jnp.minimum(b + 1, nu - 1)
        for r in range(bm):
            gather_row(nxt, 1 - slot, r).start()

        x = xbuf[slot, :, :d].astype(BF16)
        wts = xbuf[slot, :, d:]
        w_lo = wts[:, 0:1]
        w_hi = wts[:, 1:2]

        def ffn(wg_ref, wu_ref, wd_ref):
            f = wg_ref.shape[1]
            acc = None
            for c in range(f // fc):
                g = _mm(x, wg_ref[:, c * fc:(c + 1) * fc])
                u = _mm(x, wu_ref[:, c * fc:(c + 1) * fc])
                h = (_silu(g) * u).astype(BF16)
                part = _mm(h, wd_ref[c * fc:(c + 1) * fc, :])
                acc = part if acc is None else acc + part
            return acc

        ybuf[slot] = w_lo * ffn(wga_ref, wua_ref, wda_ref) + w_hi * ffn(wgb_ref, wub_ref, wdb_ref)
        for r in range(bm):
            scatter_row(b, slot, r).start()

        @pl.when(b == nu - 1)
        def _():
            wait_gather(1 - slot)
            wait_scatter(slot)

            @pl.when(b >= 1)
            def _():
                wait_scatter(1 - slot)


def _expert_call(src, blk_ea, blk_eb, n_used, tokx, wg, wu, wd):
    t, dx = tokx.shape
    d = dx - LANES
    f = wg.shape[-1]
    bm = MOE_BLOCK
    n_blocks = blk_ea.shape[0]
    amap = lambda b, s, ea, eb, nu: (ea[b], 0, 0)
    bmap = lambda b, s, ea, eb, nu: (eb[b], 0, 0)
    return pl.pallas_call(
        functools.partial(_expert_kernel, d=d, fc=512, bm=bm, t=t),
        out_shape=jax.ShapeDtypeStruct((t + 2 * bm, d), F32),
        grid_spec=pltpu.PrefetchScalarGridSpec(
            num_scalar_prefetch=4,
            grid=(n_blocks,),
            in_specs=[
                pl.BlockSpec(memory_space=pl.ANY),
                pl.BlockSpec((None, d, f), amap), pl.BlockSpec((None, d, f), amap), pl.BlockSpec((None, f, d), amap),
                pl.BlockSpec((None, d, f), bmap), pl.BlockSpec((None, d, f), bmap), pl.BlockSpec((None, f, d), bmap),
            ],
            out_specs=pl.BlockSpec(memory_space=pl.ANY),
            scratch_shapes=[pltpu.VMEM((2, bm, dx), F32), pltpu.VMEM((2, bm, d), F32),
                            pltpu.SemaphoreType.DMA((2,)), pltpu.SemaphoreType.DMA((2,))],
        ),
        compiler_params=_cparams(("arbitrary",)),
        name="experts",
    )(src, blk_ea, blk_eb, n_used, tokx, wg, wu, wd, wg, wu, wd)


def _combine_kernel(y2_ref, x1_ref, g2_ref, lng_ref, lnb_ref, o_ref, *, alpha):
    u = alpha * x1_ref[...] + (1.0 + g2_ref[...]) * y2_ref[...]
    o_ref[...] = _layer_norm(u, lng_ref[...], lnb_ref[...])


def _combine_call(y2, x1, g2, lng, lnb, alpha):
    bsz, l, d = x1.shape
    tm = min(ROW_TILE, l)
    nt = l // tm
    return pl.pallas_call(
        functools.partial(_combine_kernel, alpha=alpha),
        out_shape=jax.ShapeDtypeStruct((bsz, l, d), F32),
        grid=(bsz, nt),
        in_specs=[
            pl.BlockSpec((tm, d), lambda b, i: (b * nt + i, 0)),
            pl.BlockSpec((None, tm, d), lambda b, i: (b, i, 0)),
            pl.BlockSpec((None, 1, d), lambda b, i: (b, 0, 0)),
            pl.BlockSpec((1, d), lambda b, i: (0, 0)),
            pl.BlockSpec((1, d), lambda b, i: (0, 0)),
        ],
        out_specs=pl.BlockSpec((None, tm, d), lambda b, i: (b, i, 0)),
        compiler_params=_cparams(("arbitrary", "arbitrary")),
        name="combine",
    )(y2, x1, g2, lng, lnb)


def _pair_tables():
    ta, tb = [], []
    for g in range(N_GROUPS):
        for a in range(EXPERTS_PER_GROUP):
            for b in range(a + 1, EXPERTS_PER_GROUP):
                ta.append(g * EXPERTS_PER_GROUP + a)
                tb.append(g * EXPERTS_PER_GROUP + b)
    return np.asarray(ta, np.int32), np.asarray(tb, np.int32)


def _moe(x1, tokx, ri, cnt, g2, lng, lnb, wg, wu, wd, alpha):
    bsz, l, d = x1.shape
    t = bsz * l
    bm = MOE_BLOCK
    n_blocks = t // bm + N_PAIR_CLASSES
    counts = cnt[:N_PAIR_CLASSES, 0].astype(I32)
    nblk = (counts + bm - 1) // bm
    cend = jnp.cumsum(nblk)
    cstart = cend - nblk
    dest = cstart[ri[0]] * bm + ri[1]
    blk = jnp.arange(n_blocks, dtype=I32)
    blk_cls = jnp.minimum(jnp.sum((cend[None, :] <= blk[:, None]).astype(I32), axis=1), N_PAIR_CLASSES - 1)
    ta, tb = _pair_tables()
    blk_ea = jnp.asarray(ta)[blk_cls]
    blk_eb = jnp.asarray(tb)[blk_cls]
    n_used = cend[-1:].astype(I32)
    n_rows = n_blocks * bm
    pad_lo = jnp.concatenate([cstart * bm + counts, cend[-1:] * bm]).astype(I32)
    pad_hi = jnp.concatenate([cend * bm, jnp.full((1,), n_rows, I32)]).astype(I32)
    src = _invperm_call(dest, pad_lo, pad_hi, n_rows)
    y2 = _expert_call(src, blk_ea, blk_eb, n_used, tokx, wg, wu, wd)
    return _combine_call(y2, x1, g2, lng, lnb, alpha)


def _qkv_kernel(x_ref, sc_ref, sh_ref, w_ref, *rest, rope, groups, d, q_scale):
    if rope:
        tab_ref, out_refs = rest[0], rest[1:]
    else:
        tab_ref, out_refs = None, rest
    h = (x_ref[...] * (1.0 + sc_ref[...]) + sh_ref[...]).astype(BF16)
    for (g, transposed), o_ref in zip(groups, out_refs):
        for c in range(d // MXU_DIM):
            col = g * d + c * MXU_DIM
            z = _mm(h, w_ref[:, col:col + MXU_DIM])
            if rope and g < 2:
                heads = []
                for hh in range(MXU_DIM // LANES):
                    zh = z[:, hh * LANES:(hh + 1) * LANES]
                    heads.append(zh * tab_ref[0] + pltpu.roll(zh, LANES - 16, 1) * tab_ref[1]
                                 + pltpu.roll(zh, 16, 1) * tab_ref[2])
                z = jnp.concatenate(heads, axis=1)
            if g == 0:
                z = z * q_scale
            if transposed:
                o_ref[c * MXU_DIM:(c + 1) * MXU_DIM, :] = z.T.astype(BF16)
            else:
                o_ref[:, c * MXU_DIM:(c + 1) * MXU_DIM] = z.astype(BF16)


def _qkv_call(x, sc, sh, w, tab, groups, q_scale):
    bsz, l, d = x.shape
    tm = min(ROW_TILE, l)
    rope = tab is not None
    in_specs = [
        pl.BlockSpec((None, tm, d), lambda b, i: (b, i, 0)),
        pl.BlockSpec((None, 1, d), lambda b, i: (b, 0, 0)),
        pl.BlockSpec((None, 1, d), lambda b, i: (b, 0, 0)),
        pl.BlockSpec(w.shape, lambda b, i: (0, 0)),
    ]
    args = [x, sc, sh, w]
    if rope:
        in_specs.append(pl.BlockSpec((3, tm, LANES), lambda b, i: (0, i, 0)))
        args.append(tab)
    out_shape, out_specs = [], []
    for (_, transposed) in groups:
        if transposed:
            out_shape.append(jax.ShapeDtypeStruct((bsz, d, l), BF16))
            out_specs.append(pl.BlockSpec((None, d, tm), lambda b, i: (b, 0, i)))
        else:
            out_shape.append(jax.ShapeDtypeStruct((bsz, l, d), BF16))
            out_specs.append(pl.BlockSpec((None, tm, d), lambda b, i: (b, i, 0)))
    return pl.pallas_call(
        functools.partial(_qkv_kernel, rope=rope, groups=groups, d=d, q_scale=q_scale),
        out_shape=tuple(out_shape),
        grid=(bsz, l // tm),
        in_specs=in_specs,
        out_specs=tuple(out_specs),
        compiler_params=_cparams(("arbitrary", "arbitrary")),
        name="qkv",
    )(*args)


def _attn_kernel(q_ref, kt_ref, v_ref, kct_ref, vc_ref, lam_ref, subln_ref, o_ref, m_scr, acc_scr, s_scr, p_scr,
                 a_scr, *, tq, tk, lam_init):
    q = q_ref[...]
    hd = q.shape[1]
    lane = lax.broadcasted_iota(I32, q.shape, 1)
    zero = jnp.zeros_like(q)
    lhs = jnp.concatenate([jnp.where(lane < hd // 2, q, zero), jnp.where(lane >= hd // 2, q, zero)], axis=0)
    m_scr[...] = jnp.full_like(m_scr, -jnp.inf)
    acc_scr[...] = jnp.zeros_like(acc_scr)

    def softmax(s):
        m_old = m_scr[...]
        m_new = jnp.maximum(m_old, jnp.max(s, axis=1, keepdims=True))
        m_scr[...] = m_new
        p = jnp.exp2(s - jnp.concatenate([m_new] * (s.shape[1] // LANES), axis=1)).astype(BF16)
        return p, jnp.exp2(m_old - m_new)

    def accumulate(p, a, vb):
        tkk = vb.shape[0]
        ones_col = jnp.where(lax.broadcasted_iota(I32, (tkk, LANES), 1) == 0, 1.0, 0.0).astype(BF16)
        vext = jnp.concatenate([vb, ones_col], axis=1)
        acc_scr[...] = jnp.concatenate([a, a], axis=1) * acc_scr[...] + _mm(p, vext)

    def chunk_start(j):
        return j * tk if isinstance(j, int) else pl.multiple_of(j * tk, tk)

    def k_chunk(j):
        return kt_ref[:, pl.ds(chunk_start(j), tk)]

    def v_chunk(j):
        return v_ref[pl.ds(chunk_start(j), tk), :]

    n_chunks = v_ref.shape[0] // tk
    s_scr[0] = _mm(lhs, k_chunk(0))
    if n_chunks > 1:
        s_scr[1] = _mm(lhs, k_chunk(1))
    p0, a0 = softmax(s_scr[0])
    p_scr[0] = p0
    a_scr[0] = a0

    def stage(j, slot):
        s_next = _mm(lhs, k_chunk(j + 1))
        accumulate(p_scr[1 - slot], a_scr[1 - slot], v_chunk(j - 1))
        p, a = softmax(s_scr[slot])
        s_scr[1 - slot] = s_next
        p_scr[slot] = p
        a_scr[slot] = a

    for j in range(1, n_chunks - 1):
        stage(j, j % 2)
    last = (n_chunks - 1) % 2
    sc = _mm(lhs, kct_ref[...])
    if n_chunks > 1:
        accumulate(p_scr[1 - last], a_scr[1 - last], v_chunk(n_chunks - 2))
        p, a = softmax(s_scr[last])
    else:
        p, a = p0, a0
    accumulate(p, a, v_chunk(n_chunks - 1))
    pc, ac = softmax(sc)
    accumulate(pc, ac, vc_ref[...])

    lam = lam_ref[...]
    lam_full = (jnp.exp(jnp.sum(lam[0:1] * lam[1:2], axis=1, keepdims=True))
                - jnp.exp(jnp.sum(lam[2:3] * lam[3:4], axis=1, keepdims=True)) + lam_init)
    acc = acc_scr[...]
    o12 = acc[:, :hd] / acc[:, hd:hd + 1]
    o = o12[:tq] - lam_full * o12[tq:]
    o = o * lax.rsqrt(jnp.mean(o * o, axis=1, keepdims=True) + LN_EPS)
    o = o * (subln_ref[...] * (1.0 - lam_init))
    o_ref[...] = o.astype(BF16)


def _attn_call(q, kt, v, kct, vc, lam, subln_row, lam_init):
    bsz, l, d = q.shape
    lc = vc.shape[1]
    hd = d // ATT_HEADS
    tq = min(ATT_TQ, l)
    tk = min(ATT_TK, l)
    return pl.pallas_call(
        functools.partial(_attn_kernel, tq=tq, tk=tk, lam_init=lam_init),
        out_shape=jax.ShapeDtypeStruct((bsz, l, d), BF16),
        grid=(bsz, ATT_HEADS, l // tq),
        in_specs=[
            pl.BlockSpec((None, tq, hd), lambda b, h, i: (b, i, h)),
            pl.BlockSpec((None, hd, l), lambda b, h, i: (b, h, 0)),
            pl.BlockSpec((None, l, hd), lambda b, h, i: (b, 0, h)),
            pl.BlockSpec((None, hd, lc), lambda b, h, i: (b, h, 0)),
            pl.BlockSpec((None, lc, hd), lambda b, h, i: (b, 0, h)),
            pl.BlockSpec(lam.shape, lambda b, h, i: (0, 0)),
            pl.BlockSpec(subln_row.shape, lambda b, h, i: (0, 0)),
        ],
        out_specs=pl.BlockSpec((None, tq, hd), lambda b, h, i: (b, i, h)),
        scratch_shapes=[pltpu.VMEM((2 * tq, LANES), F32), pltpu.VMEM((2 * tq, hd + LANES), F32),
                        pltpu.VMEM((2, 2 * tq, tk), F32), pltpu.VMEM((2, 2 * tq, tk), BF16),
                        pltpu.VMEM((2, 2 * tq, LANES), F32)],
        compiler_params=_cparams(("arbitrary", "arbitrary", "arbitrary")),
        name="diff_attn",
    )(q, kt, v, kct, vc, lam, subln_row)


def _rope_tables(l, hd):
    dh = hd // 2
    half = dh // 2
    rows = l // GRID_W
    row = jnp.repeat(jnp.arange(rows), GRID_W).astype(F32)
    col = jnp.tile(jnp.arange(GRID_W), rows).astype(F32)
    inv = ROPE_THETA ** (-jnp.arange(0, half, 2, dtype=F32) / half)
    ar = row[:, None] * inv
    ac = col[:, None] * inv
    ang = jnp.concatenate([ar, ar, ac, ac], axis=-1)
    cos, sin = jnp.cos(ang), jnp.sin(ang)
    first = (np.arange(dh) % half) < half // 2
    sin_a = jnp.where(first, -sin, 0.0)
    sin_b = jnp.where(first, 0.0, sin)
    return jnp.stack([jnp.tile(cos, (1, 2)), jnp.tile(sin_a, (1, 2)), jnp.tile(sin_b, (1, 2))])


def _attention_mixer(x, ctx, sc1, sh1, csc1, csh1, wqkv, lam, subln, layer_idx):
    bsz, l, d = x.shape
    hd = d // ATT_HEADS
    lam_init = 0.8 - 0.6 * math.exp(-0.3 * layer_idx)
    w = wqkv.astype(BF16)
    q, kt, v = _qkv_call(x, sc1, sh1, w, _rope_tables(l, hd), ((0, False), (1, True), (2, False)),
                         (hd // 2) ** -0.5 * math.log2(math.e))
    ones = jnp.ones((bsz, 1, 1), F32)
    kct, vc = _qkv_call(ctx, csc1[None, None, :] * ones, csh1[None, None, :] * ones, w, None,
                        ((1, True), (2, False)), 1.0)
    return _attn_call(q, kt, v, kct, vc, lam, subln.reshape(1, hd), lam_init)


def _halo_specs(tm, d, l):
    per = tm // HALO
    last = l // HALO - 1
    return [
        pl.BlockSpec((None, tm, d), lambda b, i: (b, i, 0)),
        pl.BlockSpec((None, HALO, d), lambda b, i: (b, jnp.maximum(i * per - 1, 0), 0)),
        pl.BlockSpec((None, HALO, d), lambda b, i: (b, jnp.minimum((i + 1) * per, last), 0)),
    ]


def _pool_kernel(x_ref, xp_ref, xn_ref, sc_ref, sh_ref, o_ref, ext, *, tm, l):
    i = pl.program_id(1)
    nt = pl.num_programs(1)
    sc = 1.0 + sc_ref[...]
    sh = sh_ref[...]
    ext[HALO:HALO + tm, :] = x_ref[...] * sc + sh
    ext[0:HALO, :] = jnp.where(i > 0, xp_ref[...] * sc + sh, 0.0)
    ext[HALO + tm:, :] = jnp.where(i < nt - 1, xn_ref[...] * sc + sh, 0.0)
    gw = x_ref.shape[1] // len(POOL_WINDOWS)
    t = i * tm + lax.broadcasted_iota(I32, (tm, gw), 0)
    for g, win in enumerate(POOL_WINDOWS):
        c0 = g * gw
        acc = None
        for j in range(-(win // 2), win - win // 2):
            part = ext[HALO + j:HALO + j + tm, c0:c0 + gw]
            acc = part if acc is None else acc + part
        lo = jnp.maximum(t - win // 2, 0)
        hi = jnp.minimum(t - win // 2 + win, l)
        mean = acc / (hi - lo).astype(F32)
        o_ref[:, c0:c0 + gw] = (mean - ext[HALO:HALO + tm, c0:c0 + gw]).astype(BF16)


def _pool_call(x, sc1, sh1):
    bsz, l, d = x.shape
    tm = min(ROW_TILE, l)
    bvec = lambda: pl.BlockSpec((None, 1, d), lambda b, i: (b, 0, 0))
    return pl.pallas_call(
        functools.partial(_pool_kernel, tm=tm, l=l),
        out_shape=jax.ShapeDtypeStruct((bsz, l, d), BF16),
        grid=(bsz, l // tm),
        in_specs=_halo_specs(tm, d, l) + [bvec(), bvec()],
        out_specs=pl.BlockSpec((None, tm, d), lambda b, i: (b, i, 0)),
        scratch_shapes=[pltpu.VMEM((tm + 2 * HALO, d), F32)],
        compiler_params=_cparams(("arbitrary", "arbitrary")),
        name="pool",
    )(x, x, x, sc1, sh1)


KRON = SUBLANES
INNER = 128
N_RB = INNER // KRON


def _trig(num, den):
    ang = jnp.asarray(np.mod(num, den), F32) * (2.0 * math.pi / den)
    return jnp.cos(ang), jnp.sin(ang)


def _kron_rows(m):
    p, q = m.shape
    rep_r = jnp.asarray(np.kron(np.eye(p), np.ones((KRON, 1))), F32)
    rep_c = jnp.asarray(np.kron(np.eye(q), np.ones((1, KRON))), F32)
    hi = lax.Precision.HIGHEST
    big = jnp.dot(jnp.dot(rep_r, m, precision=hi), rep_c, precision=hi)
    r = lax.broadcasted_iota(I32, big.shape, 0) % KRON
    c = lax.broadcasted_iota(I32, big.shape, 1) % KRON
    return jnp.where(r == c, big, 0.0)


def _outer_kernel(*refs, n_in):
    m_ref, x_refs, o_ref = refs[0], refs[1:1 + n_in], refs[1 + n_in]
    acc = None
    col = 0
    for x_ref in x_refs:
        a, j, d = x_ref.shape
        x = x_ref[...].reshape(a * j, d).astype(BF16)
        part = _mm(m_ref[:, col:col + a * j], x)
        acc = part if acc is None else acc + part
        col += a * j
    o_ref[...] = acc.astype(o_ref.dtype)


def _outer_fwd_call(m, xs, a_out):
    bsz, rows, d = xs[0].shape
    a_in = rows // INNER
    xs4 = [x.reshape(bsz, a_in, N_RB, KRON, d) for x in xs]
    return pl.pallas_call(
        functools.partial(_outer_kernel, n_in=len(xs)),
        out_shape=jax.ShapeDtypeStruct((bsz, N_RB, a_out * 2 * KRON, d), BF16),
        grid=(bsz, N_RB),
        in_specs=[pl.BlockSpec(m.shape, lambda b, r: (0, 0))]
        + [pl.BlockSpec((None, a_in, None, KRON, d), lambda b, r: (b, 0, r, 0, 0)) for _ in xs],
        out_specs=pl.BlockSpec((None, None, a_out * 2 * KRON, d), lambda b, r: (b, r, 0, 0)),
        compiler_params=_cparams(("arbitrary", "arbitrary")),
        name="dft_outer",
    )(m, *xs4)


def _fn_chan_kernel(x_ref, sc_ref, sh_ref, cs_ref, pr_ref, pi_ref, *, gw):
    h = (x_ref[...] * (1.0 + sc_ref[...]) + sh_ref[...]).astype(BF16)
    for g in range(h.shape[1] // gw):
        z = _mm(h[:, g * gw:(g + 1) * gw], cs_ref[...])
        pr_ref[:, g * gw:(g + 1) * gw] = z[:, :gw]
        pi_ref[:, g * gw:(g + 1) * gw] = z[:, gw:]


def _fn_chan_call(x, sc1, sh1):
    bsz, l, d = x.shape
    gw = d // FFT_GROUPS
    tm = min(ROW_TILE, l)
    cc, ss = _trig(np.outer(np.arange(gw), np.arange(gw)), gw)
    cs = jnp.concatenate([cc, -ss], axis=1).astype(BF16)
    bvec = lambda: pl.BlockSpec((None, 1, d), lambda b, i: (b, 0, 0))
    tile = lambda: pl.BlockSpec((None, tm, d), lambda b, i: (b, i, 0))
    return pl.pallas_call(
        functools.partial(_fn_chan_kernel, gw=gw),
        out_shape=(jax.ShapeDtypeStruct((bsz, l, d), F32), jax.ShapeDtypeStruct((bsz, l, d), F32)),
        grid=(bsz, l // tm),
        in_specs=[tile(), bvec(), bvec(), pl.BlockSpec(cs.shape, lambda b, i: (0, 0))],
        out_specs=(tile(), tile()),
        compiler_params=_cparams(("arbitrary", "arbitrary")),
        name="fnet_channel_dft",
    )(x, sc1, sh1, cs)


def _fn_inner_kernel(m_ref, x_ref, o_ref):
    rb, j1, cj, d = x_ref.shape
    x = x_ref[...].reshape(rb * j1 * cj, d)
    y = _mm(m_ref[...], x)
    o_ref[...] = y.reshape(o_ref.shape)


def _fn_inner_call(m2, spec, a):
    bsz, _, _, d = spec.shape
    spec5 = spec.reshape(bsz, N_RB, a, 2 * KRON, d)
    out = pl.pallas_call(
        _fn_inner_kernel,
        out_shape=jax.ShapeDtypeStruct((bsz, INNER, a, d), F32),
        grid=(a // KRON, bsz),
        in_specs=[
            pl.BlockSpec((None,) + m2.shape[1:], lambda s, b: (s, 0, 0)),
            pl.BlockSpec((None, N_RB, KRON, 2 * KRON, d), lambda s, b: (b, 0, s, 0, 0)),
        ],
        out_specs=pl.BlockSpec((None, INNER, KRON, d), lambda s, b: (b, 0, s, 0)),
        compiler_params=_cparams(("arbitrary", "arbitrary")),
        name="fnet_inner_dft",
    )(m2, spec5)
    return out.reshape(bsz, INNER * a, d)


def _fnet_tables(l, gw):
    a = l // INNER
    k1 = np.arange(a)
    c1, s1 = _trig(np.outer(k1, k1), a)
    m1 = jnp.stack([jnp.stack([c1, s1], axis=1), jnp.stack([-s1, c1], axis=1)], axis=1)
    m1k = _kron_rows(m1.reshape(2 * a, 2 * a))
    rb, j1c, cc, jj = np.meshgrid(np.arange(N_RB), np.arange(KRON), np.arange(2), np.arange(KRON), indexing="ij")
    l2 = (rb * KRON + jj).reshape(-1)
    j1c, cc = j1c.reshape(-1), cc.reshape(-1)
    norm = 1.0 / math.sqrt(l * gw)
    ca, sa = _trig(np.outer(np.arange(INNER), l2), INNER)
    p_tab = jnp.where(cc == 0, ca, sa) * norm
    q_tab = jnp.where(cc == 0, -sa, ca) * norm
    k1_col = KRON * np.arange(a // KRON)[:, None] + j1c[None, :]
    cb, sb = _trig(k1_col * l2[None, :], l)
    t3 = p_tab[None] * cb[:, None, :] + q_tab[None] * sb[:, None, :]
    mask = jnp.asarray(j1c[None, :] == np.arange(KRON)[:, None], F32)
    m2 = (t3[:, :, None, :] * mask[None, None]).reshape(a // KRON, INNER * KRON, l2.shape[0])
    return m1k.astype(BF16), m2.astype(BF16)


def _fourier_mixer(x, sc1, sh1):
    bsz, l, d = x.shape
    a = l // INNER
    m1k, m2 = _fnet_tables(l, d // FFT_GROUPS)
    pr, pi = _fn_chan_call(x, sc1, sh1)
    spec = _outer_fwd_call(m1k, [pr, pi], a)
    return _fn_inner_call(m2, spec, a)


def _hy_front_kernel(x_ref, xp_ref, xn_ref, sc_ref, sh_ref, w_ref, bin_ref, cw_ref, cb_ref,
                     vg_ref, x0_ref, hext, zscr, *, tm, l, d):
    i = pl.program_id(1)
    sc = 1.0 + sc_ref[...]
    sh = sh_ref[...]
    hext[HALO:HALO + tm, :] = (x_ref[...] * sc + sh).astype(BF16)
    hext[0:HALO, :] = (xp_ref[...] * sc + sh).astype(BF16)
    hext[HALO + tm:, :] = (xn_ref[...] * sc + sh).astype(BF16)
    pos = i * tm - HALO + lax.broadcasted_iota(I32, (tm + 2 * HALO, MXU_DIM), 0)

    def conv(col):
        z = _mm(hext[...], w_ref[:, col:col + MXU_DIM]) + bin_ref[:, col:col + MXU_DIM]
        zscr[...] = jnp.where(pos >= 0, jnp.where(pos < l, z, 0.0), 0.0)
        cw = cw_ref[:, col:col + MXU_DIM]
        return (zscr[HALO - 1:HALO - 1 + tm, :] * cw[0:1] + zscr[HALO:HALO + tm, :] * cw[1:2]
                + zscr[HALO + 1:HALO + 1 + tm, :] * cw[2:3] + cb_ref[:, col:col + MXU_DIM])

    for c in range(d // MXU_DIM):
        cols = slice(c * MXU_DIM, (c + 1) * MXU_DIM)
        x0_ref[:, cols] = conv(c * MXU_DIM)
        x1 = conv(d + c * MXU_DIM)
        v = conv(2 * d + c * MXU_DIM)
        vg_ref[:, cols] = v * x1


def _hy_front_call(x, sc1, sh1, win, bin_, conv_w, conv_b):
    bsz, l, d = x.shape
    tm = min(ROW_TILE, l)
    n3 = win.shape[1]
    bvec = lambda: pl.BlockSpec((None, 1, d), lambda b, i: (b, 0, 0))
    full = lambda a: pl.BlockSpec(a.shape, lambda b, i: (0, 0))
    tile = lambda: pl.BlockSpec((None, tm, d), lambda b, i: (b, i, 0))
    args = (win.astype(BF16), bin_.reshape(1, n3), conv_w, conv_b.reshape(1, n3))
    return pl.pallas_call(
        functools.partial(_hy_front_kernel, tm=tm, l=l, d=d),
        out_shape=(jax.ShapeDtypeStruct((bsz, l, d), F32), jax.ShapeDtypeStruct((bsz, l, d), F32)),
        grid=(bsz, l // tm),
        in_specs=_halo_specs(tm, d, l) + [bvec(), bvec()] + [full(a) for a in args],
        out_specs=(tile(), tile()),
        scratch_shapes=[pltpu.VMEM((tm + 2 * HALO, d), BF16), pltpu.VMEM((tm + 2 * HALO, MXU_DIM), F32)],
        compiler_params=_cparams(("arbitrary", "arbitrary")),
        name="hyena_front",
    )(x, x, x, sc1, sh1, *args)


def _hy_filter_kernel(z_ref, f1_ref, fb1_ref, f2_ref, fb2_ref, f3_ref, fr_ref, del_ref, o_ref, *, tf, l):
    z = z_ref[...]
    fr = fr_ref[...]
    a = jnp.sin(fr * (_mm(z, f1_ref[...]) + fb1_ref[...]))
    a = jnp.sin(fr * (_mm(a, f2_ref[...]) + fb2_ref[...]))
    k = _mm(a, f3_ref[...])
    window = jnp.exp(-z[:, 0:1] * del_ref[...])
    row = pl.program_id(0) * tf + lax.broadcasted_iota(I32, k.shape, 0)
    o_ref[...] = jnp.where(row == l, 0.0, k * window)


def _hy_filter_call(l, d, f1, fb1, f2, fb2, f3, freq):
    n = 2 * l
    tf = min(ROW_TILE, l)
    hid = f1.shape[1]
    pos = jnp.concatenate([jnp.arange(l, dtype=F32), jnp.zeros((1,), F32),
                           jnp.arange(l - 1, 0, -1, dtype=F32)])
    bands = jnp.linspace(1e-4, HY_BANDS - 1, HY_BANDS, dtype=F32)
    ang = (2.0 * math.pi / l) * pos[:, None] * bands[None, :]
    feat = jnp.concatenate([(pos / l)[:, None], jnp.cos(ang), jnp.sin(ang)], axis=-1)
    kin = 64
    feat = jnp.pad(feat, ((0, 0), (0, kin - feat.shape[1])))
    f1p = jnp.pad(f1, ((0, kin - f1.shape[0]), (0, 0)))
    deltas = jnp.abs(jnp.linspace(math.log(HY_TARGET) / HY_SLOW, math.log(HY_TARGET) / HY_FAST, d,
                                  dtype=F32)).reshape(1, d)
    half = l // tf
    full = lambda a: pl.BlockSpec(a.shape, lambda i: (0, 0))
    args = (f1p, fb1.reshape(1, hid), f2, fb2.reshape(1, hid))
    tail = (freq.reshape(1, hid), jnp.asarray(deltas))
    return pl.pallas_call(
        functools.partial(_hy_filter_kernel, tf=tf, l=l),
        out_shape=jax.ShapeDtypeStruct((n, d), F32),
        grid=(n // tf,),
        in_specs=[pl.BlockSpec((tf, kin), lambda i: (i, 0))] + [full(a) for a in args]
        + [pl.BlockSpec((hid, d), lambda i: (0, i // half))] + [full(a) for a in tail],
        out_specs=pl.BlockSpec((tf, d), lambda i: (i, 0)),
        compiler_params=_cparams(("arbitrary",)),
        name="hyena_filter",
    )(feat, *args, f3, *tail)


def _hy_kspec_kernel(mf_ref, x_ref, o_ref):
    rb, cj, d = x_ref.shape
    o_ref[...] = _mm(mf_ref[...], x_ref[...].reshape(rb * cj, d))


def _hy_kspec_call(mf2, spec, a):
    _, _, _, d = spec.shape
    spec5 = spec.reshape(1, N_RB, a, 2 * KRON, d)
    return pl.pallas_call(
        _hy_kspec_kernel,
        out_shape=jax.ShapeDtypeStruct((a, 2 * INNER, d), F32),
        grid=(a,),
        in_specs=[
            pl.BlockSpec((None, 2 * INNER, 2 * INNER), lambda k: (k, 0, 0)),
            pl.BlockSpec((None, N_RB, None, 2 * KRON, d), lambda k: (0, 0, k, 0, 0)),
        ],
        out_specs=pl.BlockSpec((None, 2 * INNER, d), lambda k: (k, 0, 0)),
        compiler_params=_cparams(("arbitrary",)),
        name="hyena_filter_spectrum",
    )(mf2, spec5)


def _hy_mid_kernel(mf_ref, mi_ref, k_ref, x_ref, o_ref):
    bsz, rb, cj, d = x_ref.shape
    kk = k_ref[...]
    kr, ki = kk[:INNER], kk[INNER:]
    for b in range(bsz):
        v = _mm(mf_ref[...], x_ref[b].reshape(rb * cj, d))
        vr, vi = v[:INNER], v[INNER:]
        z = jnp.concatenate([vr * kr - vi * ki, vr * ki + vi * kr], axis=0).astype(BF16)
        y = _mm(mi_ref[...], z)
        o_ref[b] = y.astype(BF16).reshape(rb, cj, d)


def _hy_mid_call(mf2, mi1, kspec, spec, a):
    bsz, _, _, d = spec.shape
    spec5 = spec.reshape(bsz, N_RB, a, 2 * KRON, d)
    mat = lambda: pl.BlockSpec((None, 2 * INNER, 2 * INNER), lambda k: (k, 0, 0))
    blk = lambda: pl.BlockSpec((bsz, N_RB, None, 2 * KRON, d), lambda k: (0, 0, k, 0, 0))
    out = pl.pallas_call(
        _hy_mid_kernel,
        out_shape=jax.ShapeDtypeStruct(spec5.shape, BF16),
        grid=(a,),
        in_specs=[mat(), mat(), pl.BlockSpec((None, 2 * INNER, d), lambda k: (k, 0, 0)), blk()],
        out_specs=blk(),
        compiler_params=_cparams(("arbitrary",)),
        name="hyena_spectral_product",
    )(mf2, mi1, kspec, spec5)
    return out.reshape(spec.shape)


def _hy_out_kernel(m_ref, b_ref, vg_ref, x0_ref, skip_ref, o_ref):
    a2, j, d = vg_ref.shape
    y = _mm(m_ref[...], b_ref[...])
    vg = vg_ref[...].reshape(a2 * j, d)
    x0 = x0_ref[...].reshape(a2 * j, d)
    o_ref[...] = ((y + vg * skip_ref[...]) * x0).reshape(a2, j, d)


def _hy_out_call(mi2, spec, vg, x0, skip):
    bsz, l, d = vg.shape
    a2 = l // INNER
    view = lambda t: t.reshape(bsz, a2, N_RB, KRON, d)
    blk = lambda: pl.BlockSpec((None, a2, None, KRON, d), lambda b, r: (b, 0, r, 0, 0))
    out = pl.pallas_call(
        _hy_out_kernel,
        out_shape=jax.ShapeDtypeStruct((bsz, a2, N_RB, KRON, d), F32),
        grid=(bsz, N_RB),
        in_specs=[
            pl.BlockSpec(mi2.shape, lambda b, r: (0, 0)),
            pl.BlockSpec((None, None) + spec.shape[2:], lambda b, r: (b, r, 0, 0)),
            blk(), blk(),
            pl.BlockSpec((1, d), lambda b, r: (0, 0)),
        ],
        out_specs=blk(),
        compiler_params=_cparams(("arbitrary", "arbitrary")),
        name="hyena_inverse_outer",
    )(mi2, spec, view(vg), view(x0), skip.reshape(1, d))
    return out.reshape(bsz, l, d)


def _hyena_tables(l):
    n = 2 * l
    a = n // INNER
    kb = np.arange(a)
    cf, sf = _trig(np.outer(kb, kb), a)
    fwd = jnp.stack([cf, -sf], axis=1)
    mf1 = _kron_rows(fwd[:, :, :a // 2].reshape(2 * a, a // 2))
    mk1 = _kron_rows(fwd.reshape(2 * a, a))
    inv = jnp.stack([cf, -sf], axis=2)[:a // 2] * (1.0 / n)
    mi2 = _kron_rows(inv.reshape(a // 2, 2 * a))
    rb, cc, jj = np.meshgrid(np.arange(N_RB), np.arange(2), np.arange(KRON), indexing="ij")
    li = (rb * KRON + jj).reshape(-1)
    cc = cc.reshape(-1)
    ca, sa = _trig(np.outer(np.arange(INNER), li), INNER)
    p_tab = jnp.concatenate([jnp.where(cc == 0, ca, sa), jnp.where(cc == 0, -sa, ca)], axis=0)
    q_tab = jnp.concatenate([jnp.where(cc == 0, -sa, ca), jnp.where(cc == 0, -ca, -sa)], axis=0)
    cb, sb = _trig(np.outer(kb, li), n)
    mf2 = (p_tab[None] * cb[:, None, :] + q_tab[None] * sb[:, None, :]).astype(BF16)
    mi1 = (p_tab.T[None] * cb[:, :, None] + q_tab.T[None] * sb[:, :, None]).astype(BF16)
    return (mf1.astype(BF16), mk1.astype(BF16), mi2.astype(BF16), mf2, mi1)


def _hyena_mixer(x, sc1, sh1, win, bin_, conv_w, conv_b, f1, fb1, f2, fb2, f3, freq, skip):
    bsz, l, d = x.shape
    a = 2 * l // INNER
    mf1, mk1, mi2, mf2, mi1 = _hyena_tables(l)
    vg, x0 = _hy_front_call(x, sc1, sh1, win, bin_, conv_w, conv_b)
    kfull = _hy_filter_call(l, d, f1, fb1, f2, fb2, f3, freq)
    kspec = _hy_kspec_call(mf2, _outer_fwd_call(mk1, [kfull[None]], a), a)
    spec = _outer_fwd_call(mf1, [vg], a)
    spec = _hy_mid_call(mf2, mi1, kspec, spec, a)
    return _hy_out_call(mi2, spec, vg, x0, skip)


def kernel(x, c, ctx, c_ctx, mod_w, mod_b, ln_g, ln_b, attn_wqkv, attn_wo, attn_lam, attn_subln, pool_w,
           pool_scale, fnet_w, fnet_b, hy_win, hy_bin, hy_conv_w, hy_conv_b, hy_f1, hy_fb1, hy_f2, hy_fb2,
           hy_f3, hy_freq, hy_skip, hy_wo, hy_bo, router_w, router_b, moe_wg, moe_wu, moe_wd):
    bsz, l, d = x.shape
    depth = mod_w.shape[0]
    n_mixers = 4
    assert depth <= n_mixers
    assert bsz + 1 <= SUBLANES
    alpha = (2 * depth) ** 0.25
    c8 = jnp.zeros((SUBLANES, d), F32).at[:bsz].set(c).at[bsz].set(c_ctx)
    mod = _mod_call(c8, mod_w, mod_b)
    rwt = router_w.T.astype(BF16)
    rb = router_b.reshape(-1, 1)
    zeros = jnp.zeros((1, d), F32)
    ones = jnp.ones((1, d), F32)
    xl = x
    for i in range(depth):
        kind, j = i % n_mixers, i // n_mixers
        ml = mod[i, :bsz]
        sh1, sc1, g1, sh2, sc2, g2 = [ml[:, k * d:(k + 1) * d][:, None, :] for k in range(6)]
        bias, scale = zeros, ones
        if kind == 0:
            mc = mod[i, bsz]
            a = _attention_mixer(xl, ctx, sc1, sh1, mc[d:2 * d], mc[:d], attn_wqkv[j], attn_lam[j],
                                 attn_subln[j], i)
            w = attn_wo[j]
        elif kind == 1:
            a = _pool_call(xl, sc1, sh1)
            w = jax.scipy.linalg.block_diag(*[pool_w[j, g] for g in range(pool_w.shape[1])])
            scale = pool_scale[j].reshape(1, d)
        elif kind == 2:
            a = _fourier_mixer(xl, sc1, sh1)
            w, bias = fnet_w[j], fnet_b[j].reshape(1, d)
        else:
            a = _hyena_mixer(xl, sc1, sh1, hy_win[j], hy_bin[j], hy_conv_w[j], hy_conv_b[j], hy_f1[j],
                             hy_fb1[j], hy_f2[j], hy_fb2[j], hy_f3[j], hy_freq[j], hy_skip[j])
            w, bias = hy_wo[j], hy_bo[j].reshape(1, d)
        x1, tokx, ri, cnt = _post_call(a, w.astype(BF16), bias, scale, xl, g1, ln_g[i, 0:1], ln_b[i, 0:1],
                                       sc2, sh2, rwt, rb, alpha)
        xl = _moe(x1, tokx, ri, cnt, g2, ln_g[i, 1:2], ln_b[i, 1:2], moe_wg[i].astype(BF16),
                  moe_wu[i].astype(BF16), moe_wd[i].astype(BF16), alpha)
    return xl
```

```python
import functools
import math

import numpy as np
import jax
import jax.numpy as jnp
from jax import lax
from jax.experimental import pallas as pl
from jax.experimental.pallas import tpu as pltpu

F32 = jnp.float32
BF16 = jnp.bfloat16
I32 = jnp.int32

GRID_W = 64
ATT_HEADS = 8
ROPE_THETA = 10000.0
POOL_WINDOWS = (2, 4, 8, 16)
FFT_GROUPS = 4
HY_BANDS = 16
HY_TARGET = 1e-2
HY_FAST = 0.3
HY_SLOW = 1.5
N_GROUPS = 4
EXPERTS_PER_GROUP = 4
LN_EPS = 1e-5

LANES = 128
SUBLANES = 8
MXU_DIM = 256
VMEM_LIMIT = 56 * 1024 * 1024

ROW_TILE = 512
MOE_BLOCK = 256
N_PAIR_CLASSES = 24
CLASS_ROWS = 32
ATT_TQ = 256
ATT_TK = 512
HALO = 16


def _cparams(sem):
    return pltpu.CompilerParams(dimension_semantics=sem, vmem_limit_bytes=VMEM_LIMIT)


def _mm(a, b):
    return jnp.dot(a, b, preferred_element_type=F32)


def _mm_nt(a, b):
    return lax.dot_general(a, b, (((1,), (1,)), ((), ())), preferred_element_type=F32)


def _silu(x):
    return x * jax.nn.sigmoid(x)


def _layer_norm(u, g, b):
    mu = jnp.mean(u, axis=-1, keepdims=True)
    d = u - mu
    var = jnp.mean(d * d, axis=-1, keepdims=True)
    return d * lax.rsqrt(var + LN_EPS) * g + b


def _mod_kernel(c_ref, w_ref, b_ref, o_ref):
    s = _silu(c_ref[...]).astype(BF16)
    o_ref[0] = _mm(s, w_ref[0].astype(BF16)) + b_ref[0]


def _mod_call(c8, mod_w, mod_b):
    depth, d, n = mod_w.shape
    tn = 1536
    return pl.pallas_call(
        _mod_kernel,
        out_shape=jax.ShapeDtypeStruct((depth, 8, n), F32),
        grid=(depth, n // tn),
        in_specs=[
            pl.BlockSpec((8, d), lambda i, j: (0, 0)),
            pl.BlockSpec((1, d, tn), lambda i, j: (i, 0, j)),
            pl.BlockSpec((1, 1, tn), lambda i, j: (i, 0, j)),
        ],
        out_specs=pl.BlockSpec((1, 8, tn), lambda i, j: (i, 0, j)),
        compiler_params=_cparams(("arbitrary", "arbitrary")),
        name="mod",
    )(c8, mod_w, mod_b.reshape(depth, 1, n))


def _route_rows(logits_t):
    m = jnp.max(logits_t, axis=0, keepdims=True)
    e = jnp.exp(logits_t - m)
    p = e / jnp.sum(e, axis=0, keepdims=True)
    r = [p[i:i + 1, :] for i in range(N_GROUPS * EXPERTS_PER_GROUP)]
    gs = []
    for g in range(N_GROUPS):
        q = r[4 * g:4 * g + 4]
        best = q[0] + q[1]
        for (i, j) in ((0, 2), (0, 3), (1, 2), (1, 3), (2, 3)):
            best = jnp.maximum(best, q[i] + q[j])
        gs.append(best)
    gbest, gidx = gs[0], jnp.zeros_like(gs[0], dtype=I32)
    for g in range(1, N_GROUPS):
        upd = gs[g] > gbest
        gidx = jnp.where(upd, g, gidx)
        gbest = jnp.where(upd, gs[g], gbest)
    v = []
    for j in range(EXPERTS_PER_GROUP):
        vj = r[j]
        for g in range(1, N_GROUPS):
            vj = jnp.where(gidx == g, r[4 * g + j], vj)
        v.append(vj)

    def first_argmax(vals):
        best, idx = vals[0], jnp.zeros_like(gidx)
        for j in range(1, len(vals)):
            upd = vals[j] > best
            idx = jnp.where(upd, j, idx)
            best = jnp.where(upd, vals[j], best)
        return best, idx

    v1, i1 = first_argmax(v)
    v2, i2 = first_argmax([jnp.where(i1 == j, -1.0, v[j]) for j in range(EXPERTS_PER_GROUP)])
    tot = v1 + v2
    w1, w2 = v1 / tot, v2 / tot
    lo = jnp.minimum(i1, i2)
    hi = jnp.maximum(i1, i2)
    pair = jnp.where(lo == 0, hi - 1, jnp.where(lo == 1, hi + 1, 5))
    cls = gidx * 6 + pair
    first_is_lo = i1 < i2
    w_lo = jnp.where(first_is_lo, w1, w2)
    w_hi = jnp.where(first_is_lo, w2, w1)
    return cls, w_lo, w_hi


def _post_kernel(a_ref, w_ref, bias_ref, scale_ref, x_ref, g1_ref, lng_ref, lnb_ref, sc2_ref, sh2_ref,
                 rwt_ref, rb_ref, tri_ref, x1_ref, tokx_ref, ri_ref, cnt_ref, cnt_scr, *, alpha, d):
    first = jnp.logical_and(pl.program_id(0) == 0, pl.program_id(1) == 0)

    @pl.when(first)
    def _():
        cnt_scr[...] = jnp.zeros_like(cnt_scr)

    y = _mm(a_ref[...].astype(BF16), w_ref[...])
    y = (y + bias_ref[...]) * scale_ref[...]
    u = alpha * x_ref[...] + (1.0 + g1_ref[...]) * y
    x1 = _layer_norm(u, lng_ref[...], lnb_ref[...])
    x1_ref[...] = x1
    tok = x1 * (1.0 + sc2_ref[...]) + sh2_ref[...]
    tokx_ref[:, :d] = tok
    tm = tok.shape[0]
    logits_t = _mm_nt(rwt_ref[...], tok.astype(BF16)) + rb_ref[...]
    cls, w_lo, w_hi = _route_rows(logits_t)
    crow = lax.broadcasted_iota(I32, (CLASS_ROWS, tm), 0)
    onehot = (crow == cls).astype(F32)
    prefix = _mm(onehot.astype(BF16), tri_ref[...])
    base = cnt_scr[...]
    rank = jnp.sum(onehot * (prefix + base), axis=0, keepdims=True).astype(I32)
    cnt_new = base + jnp.sum(onehot, axis=1, keepdims=True)
    cnt_scr[...] = cnt_new
    cnt_ref[...] = cnt_new[:, :LANES]
    r8 = lax.broadcasted_iota(I32, (SUBLANES, tm), 0)
    ri_ref[...] = jnp.where(r8 == 0, cls, jnp.where(r8 == 1, rank, 0))
    r128 = lax.broadcasted_iota(I32, (LANES, tm), 0)
    wmat = jnp.where(r128 == 0, w_lo, jnp.where(r128 == 1, w_hi, 0.0))
    tokx_ref[:, d:] = wmat.T


def _post_call(a, w, bias, scale, x, g1, lng, lnb, sc2, sh2, rwt, rb, alpha):
    bsz, l, d = x.shape
    k = a.shape[-1]
    tm = min(ROW_TILE, l)
    nt = l // tm
    t = bsz * l
    tri = jnp.asarray(np.triu(np.ones((tm, tm), np.float32), 1), BF16)
    vec = lambda: pl.BlockSpec((1, d), lambda b, i: (0, 0))
    bvec = lambda: pl.BlockSpec((None, 1, d), lambda b, i: (b, 0, 0))
    return pl.pallas_call(
        functools.partial(_post_kernel, alpha=alpha, d=d),
        out_shape=(
            jax.ShapeDtypeStruct((bsz, l, d), F32),
            jax.ShapeDtypeStruct((t, d + LANES), F32),
            jax.ShapeDtypeStruct((SUBLANES, t), I32),
            jax.ShapeDtypeStruct((CLASS_ROWS, LANES), F32),
        ),
        grid=(bsz, nt),
        in_specs=[
            pl.BlockSpec((None, tm, k), lambda b, i: (b, i, 0)),
            pl.BlockSpec((k, d), lambda b, i: (0, 0)),
            vec(), vec(),
            pl.BlockSpec((None, tm, d), lambda b, i: (b, i, 0)),
            bvec(), vec(), vec(), bvec(), bvec(),
            pl.BlockSpec(rwt.shape, lambda b, i: (0, 0)),
            pl.BlockSpec(rb.shape, lambda b, i: (0, 0)),
            pl.BlockSpec((tm, tm), lambda b, i: (0, 0)),
        ],
        out_specs=(
            pl.BlockSpec((None, tm, d), lambda b, i: (b, i, 0)),
            pl.BlockSpec((tm, d + LANES), lambda b, i: (b * nt + i, 0)),
            pl.BlockSpec((SUBLANES, tm), lambda b, i: (0, b * nt + i)),
            pl.BlockSpec((CLASS_ROWS, LANES), lambda b, i: (0, 0)),
        ),
        scratch_shapes=[pltpu.VMEM((CLASS_ROWS, tm), F32)],
        compiler_params=_cparams(("arbitrary", "arbitrary")),
        name="post",
    )(a, w, bias, scale, x, g1, lng, lnb, sc2, sh2, rwt, rb, tri)


def _invperm_kernel(dest_ref, pad_lo_ref, pad_hi_ref, src_ref, *, t, chunk, n_spans):
    step = pl.program_id(0)

    @pl.when(step == 0)
    def _():
        def fill(r, c):
            src_ref[r] = t + (r & (chunk - 1))
            return c
        for span in range(n_spans):
            lax.fori_loop(pad_lo_ref[span], pad_hi_ref[span], fill, 0)

    @pl.when(step > 0)
    def _():
        base = (step - 1) * chunk

        def put(i, c):
            src_ref[dest_ref[base + i]] = base + i
            return c
        lax.fori_loop(0, chunk, put, 0, unroll=8)


def _invperm_call(dest, pad_lo, pad_hi, n_rows):
    t = dest.shape[0]
    chunk = 2 * MOE_BLOCK
    assert t % chunk == 0 and chunk & (chunk - 1) == 0
    smem = lambda: pl.BlockSpec(memory_space=pltpu.SMEM)
    return pl.pallas_call(
        functools.partial(_invperm_kernel, t=t, chunk=chunk, n_spans=pad_lo.shape[0]),
        out_shape=jax.ShapeDtypeStruct((n_rows,), I32),
        grid=(1 + t // chunk,),
        in_specs=[smem(), smem(), smem()],
        out_specs=smem(),
        compiler_params=_cparams(("arbitrary",)),
        name="moe_invperm",
    )(dest, pad_lo, pad_hi)


def _expert_kernel(src_ref, ea_ref, eb_ref, nu_ref, tok_hbm, wga_ref, wua_ref, wda_ref, wgb_ref, wub_ref,
                   wdb_ref, y_hbm, xbuf, ybuf, gsem, ssem, *, d, fc, bm, t):
    del ea_ref, eb_ref
    b = pl.program_id(0)
    nu = nu_ref[0]
    slot = b % 2

    def gather_row(blk, s, r):
        tok = jnp.minimum(src_ref[blk * bm + r], t - 1)
        return pltpu.make_async_copy(tok_hbm.at[pl.ds(tok, 1)], xbuf.at[s, pl.ds(r, 1)], gsem.at[s])

    def scatter_row(blk, s, r):
        return pltpu.make_async_copy(ybuf.at[s, pl.ds(r, 1)], y_hbm.at[pl.ds(src_ref[blk * bm + r], 1)],
                                     ssem.at[s])

    def wait_gather(s):
        pltpu.make_async_copy(tok_hbm.at[pl.ds(0, bm)], xbuf.at[s], gsem.at[s]).wait()

    def wait_scatter(s):
        pltpu.make_async_copy(ybuf.at[s], y_hbm.at[pl.ds(0, bm)], ssem.at[s]).wait()

    @pl.when(b == 0)
    def _():
        ybuf[...] = jnp.zeros_like(ybuf)
        for s in range(2):
            cp = pltpu.make_async_copy(ybuf.at[s], y_hbm.at[pl.ds(t + s * bm, bm)], ssem.at[s])
            cp.start()
            cp.wait()
        for r in range(bm):
            gather_row(0, 0, r).start()

    @pl.when(b < nu)
    def _():
        wait_gather(slot)

        @pl.when(b >= 2)
        def _():
            wait_scatter(slot)

        nxt = jnp.minimum(b + 1, nu - 1)
        for r in range(bm):
            gather_row(nxt, 1 - slot, r).start(priority=r % 2)

        x = xbuf[slot, :, :d].astype(BF16)
        wts = xbuf[slot, :, d:]
        w_lo = wts[:, 0:1]
        w_hi = wts[:, 1:2]

        def ffn(wg_ref, wu_ref, wd_ref):
            f = wg_ref.shape[1]
            acc = None
            for c in range(f // fc):
                g = _mm(x, wg_ref[:, c * fc:(c + 1) * fc])
                u = _mm(x, wu_ref[:, c * fc:(c + 1) * fc])
                h = (_silu(g) * u).astype(BF16)
                part = _mm(h, wd_ref[c * fc:(c + 1) * fc, :])
                acc = part if acc is None else acc + part
            return acc

        ybuf[slot] = w_lo * ffn(wga_ref, wua_ref, wda_ref) + w_hi * ffn(wgb_ref, wub_ref, wdb_ref)
        for r in range(bm):
            scatter_row(b, slot, r).start(priority=r % 2)

        @pl.when(b == nu - 1)
        def _():
            wait_gather(1 - slot)
            wait_scatter(slot)

            @pl.when(b >= 1)
            def _():
                wait_scatter(1 - slot)


def _expert_call(src, blk_ea, blk_eb, n_used, tokx, wg, wu, wd):
    t, dx = tokx.shape
    d = dx - LANES
    f = wg.shape[-1]
    bm = MOE_BLOCK
    n_blocks = blk_ea.shape[0]
    amap = lambda b, s, ea, eb, nu: (ea[b], 0, 0)
    bmap = lambda b, s, ea, eb, nu: (eb[b], 0, 0)
    return pl.pallas_call(
        functools.partial(_expert_kernel, d=d, fc=512, bm=bm, t=t),
        out_shape=jax.ShapeDtypeStruct((t + 2 * bm, d), F32),
        grid_spec=pltpu.PrefetchScalarGridSpec(
            num_scalar_prefetch=4,
            grid=(n_blocks,),
            in_specs=[
                pl.BlockSpec(memory_space=pl.ANY),
                pl.BlockSpec((None, d, f), amap), pl.BlockSpec((None, d, f), amap), pl.BlockSpec((None, f, d), amap),
                pl.BlockSpec((None, d, f), bmap), pl.BlockSpec((None, d, f), bmap), pl.BlockSpec((None, f, d), bmap),
            ],
            out_specs=pl.BlockSpec(memory_space=pl.ANY),
            scratch_shapes=[pltpu.VMEM((2, bm, dx), F32), pltpu.VMEM((2, bm, d), F32),
                            pltpu.SemaphoreType.DMA((2,)), pltpu.SemaphoreType.DMA((2,))],
        ),
        compiler_params=_cparams(("arbitrary",)),
        name="experts",
    )(src, blk_ea, blk_eb, n_used, tokx, wg, wu, wd, wg, wu, wd)


def _combine_kernel(y2_ref, x1_ref, g2_ref, lng_ref, lnb_ref, o_ref, *, alpha):
    u = alpha * x1_ref[...] + (1.0 + g2_ref[...]) * y2_ref[...]
    o_ref[...] = _layer_norm(u, lng_ref[...], lnb_ref[...])


def _combine_call(y2, x1, g2, lng, lnb, alpha):
    bsz, l, d = x1.shape
    tm = min(ROW_TILE, l)
    nt = l // tm
    return pl.pallas_call(
        functools.partial(_combine_kernel, alpha=alpha),
        out_shape=jax.ShapeDtypeStruct((bsz, l, d), F32),
        grid=(bsz, nt),
        in_specs=[
            pl.BlockSpec((tm, d), lambda b, i: (b * nt + i, 0)),
            pl.BlockSpec((None, tm, d), lambda b, i: (b, i, 0)),
            pl.BlockSpec((None, 1, d), lambda b, i: (b, 0, 0)),
            pl.BlockSpec((1, d), lambda b, i: (0, 0)),
            pl.BlockSpec((1, d), lambda b, i: (0, 0)),
        ],
        out_specs=pl.BlockSpec((None, tm, d), lambda b, i: (b, i, 0)),
        compiler_params=_cparams(("arbitrary", "arbitrary")),
        name="combine",
    )(y2, x1, g2, lng, lnb)


def _pair_tables():
    ta, tb = [], []
    for g in range(N_GROUPS):
        for a in range(EXPERTS_PER_GROUP):
            for b in range(a + 1, EXPERTS_PER_GROUP):
                ta.append(g * EXPERTS_PER_GROUP + a)
                tb.append(g * EXPERTS_PER_GROUP + b)
    return np.asarray(ta, np.int32), np.asarray(tb, np.int32)


def _moe(x1, tokx, ri, cnt, g2, lng, lnb, wg, wu, wd, alpha):
    bsz, l, d = x1.shape
    t = bsz * l
    bm = MOE_BLOCK
    n_blocks = t // bm + N_PAIR_CLASSES
    counts = cnt[:N_PAIR_CLASSES, 0].astype(I32)
    nblk = (counts + bm - 1) // bm
    cend = jnp.cumsum(nblk)
    cstart = cend - nblk
    dest = cstart[ri[0]] * bm + ri[1]
    blk = jnp.arange(n_blocks, dtype=I32)
    blk_cls = jnp.minimum(jnp.sum((cend[None, :] <= blk[:, None]).astype(I32), axis=1), N_PAIR_CLASSES - 1)
    ta, tb = _pair_tables()
    blk_ea = jnp.asarray(ta)[blk_cls]
    blk_eb = jnp.asarray(tb)[blk_cls]
    n_used = cend[-1:].astype(I32)
    n_rows = n_blocks * bm
    pad_lo = jnp.concatenate([cstart * bm + counts, cend[-1:] * bm]).astype(I32)
    pad_hi = jnp.concatenate([cend * bm, jnp.full((1,), n_rows, I32)]).astype(I32)
    src = _invperm_call(dest, pad_lo, pad_hi, n_rows)
    y2 = _expert_call(src, blk_ea, blk_eb, n_used, tokx, wg, wu, wd)
    return _combine_call(y2, x1, g2, lng, lnb, alpha)


def _qkv_kernel(x_ref, sc_ref, sh_ref, w_ref, *rest, rope, groups, d, q_scale):
    if rope:
        tab_ref, out_refs = rest[0], rest[1:]
    else:
        tab_ref, out_refs = None, rest
    h = (x_ref[...] * (1.0 + sc_ref[...]) + sh_ref[...]).astype(BF16)
    for (g, transposed), o_ref in zip(groups, out_refs):
        for c in range(d // MXU_DIM):
            col = g * d + c * MXU_DIM
            z = _mm(h, w_ref[:, col:col + MXU_DIM])
            if rope and g < 2:
                heads = []
                for hh in range(MXU_DIM // LANES):
                    zh = z[:, hh * LANES:(hh + 1) * LANES]
                    heads.append(zh * tab_ref[0] + pltpu.roll(zh, LANES - 16, 1) * tab_ref[1]
                                 + pltpu.roll(zh, 16, 1) * tab_ref[2])
                z = jnp.concatenate(heads, axis=1)
            if g == 0:
                z = z * q_scale
            if transposed:
                o_ref[c * MXU_DIM:(c + 1) * MXU_DIM, :] = z.T.astype(BF16)
            else:
                o_ref[:, c * MXU_DIM:(c + 1) * MXU_DIM] = z.astype(BF16)


def _qkv_call(x, sc, sh, w, tab, groups, q_scale):
    bsz, l, d = x.shape
    tm = min(ROW_TILE, l)
    rope = tab is not None
    in_specs = [
        pl.BlockSpec((None, tm, d), lambda b, i: (b, i, 0)),
        pl.BlockSpec((None, 1, d), lambda b, i: (b, 0, 0)),
        pl.BlockSpec((None, 1, d), lambda b, i: (b, 0, 0)),
        pl.BlockSpec(w.shape, lambda b, i: (0, 0)),
    ]
    args = [x, sc, sh, w]
    if rope:
        in_specs.append(pl.BlockSpec((3, tm, LANES), lambda b, i: (0, i, 0)))
        args.append(tab)
    out_shape, out_specs = [], []
    for (_, transposed) in groups:
        if transposed:
            out_shape.append(jax.ShapeDtypeStruct((bsz, d, l), BF16))
            out_specs.append(pl.BlockSpec((None, d, tm), lambda b, i: (b, 0, i)))
        else:
            out_shape.append(jax.ShapeDtypeStruct((bsz, l, d), BF16))
            out_specs.append(pl.BlockSpec((None, tm, d), lambda b, i: (b, i, 0)))
    return pl.pallas_call(
        functools.partial(_qkv_kernel, rope=rope, groups=groups, d=d, q_scale=q_scale),
        out_shape=tuple(out_shape),
        grid=(bsz, l // tm),
        in_specs=in_specs,
        out_specs=tuple(out_specs),
        compiler_params=_cparams(("arbitrary", "arbitrary")),
        name="qkv",
    )(*args)


def _attn_kernel(q_ref, kt_ref, v_ref, kct_ref, vc_ref, lam_ref, subln_ref, o_ref, m_scr, acc_scr, s_scr, p_scr,
                 a_scr, *, tq, tk, lam_init):
    q = q_ref[...]
    hd = q.shape[1]
    lane = lax.broadcasted_iota(I32, q.shape, 1)
    zero = jnp.zeros_like(q)
    lhs = jnp.concatenate([jnp.where(lane < hd // 2, q, zero), jnp.where(lane >= hd // 2, q, zero)], axis=0)
    m_scr[...] = jnp.full_like(m_scr, -jnp.inf)
    acc_scr[...] = jnp.zeros_like(acc_scr)

    def softmax(s):
        m_old = m_scr[...]
        m_new = jnp.maximum(m_old, jnp.max(s, axis=1, keepdims=True))
        m_scr[...] = m_new
        p = jnp.exp2(s - jnp.concatenate([m_new] * (s.shape[1] // LANES), axis=1)).astype(BF16)
        return p, jnp.exp2(m_old - m_new)

    def accumulate(p, a, vb):
        tkk = vb.shape[0]
        ones_col = jnp.where(lax.broadcasted_iota(I32, (tkk, LANES), 1) == 0, 1.0, 0.0).astype(BF16)
        vext = jnp.concatenate([vb, ones_col], axis=1)
        acc_scr[...] = jnp.concatenate([a, a], axis=1) * acc_scr[...] + _mm(p, vext)

    def chunk_start(j):
        return j * tk if isinstance(j, int) else pl.multiple_of(j * tk, tk)

    def k_chunk(j):
        return kt_ref[:, pl.ds(chunk_start(j), tk)]

    def v_chunk(j):
        return v_ref[pl.ds(chunk_start(j), tk), :]

    n_chunks = v_ref.shape[0] // tk
    s_scr[0] = _mm(lhs, k_chunk(0))
    if n_chunks > 1:
        s_scr[1] = _mm(lhs, k_chunk(1))
    p0, a0 = softmax(s_scr[0])
    p_scr[0] = p0
    a_scr[0] = a0

    def stage(j, slot):
        s_next = _mm(lhs, k_chunk(j + 1))
        accumulate(p_scr[1 - slot], a_scr[1 - slot], v_chunk(j - 1))
        p, a = softmax(s_scr[slot])
        s_scr[1 - slot] = s_next
        p_scr[slot] = p
        a_scr[slot] = a

    for j in range(1, n_chunks - 1):
        stage(j, j % 2)
    last = (n_chunks - 1) % 2
    sc = _mm(lhs, kct_ref[...])
    if n_chunks > 1:
        accumulate(p_scr[1 - last], a_scr[1 - last], v_chunk(n_chunks - 2))
        p, a = softmax(s_scr[last])
    else:
        p, a = p0, a0
    accumulate(p, a, v_chunk(n_chunks - 1))
    pc, ac = softmax(sc)
    accumulate(pc, ac, vc_ref[...])

    lam = lam_ref[...]
    lam_full = (jnp.exp(jnp.sum(lam[0:1] * lam[1:2], axis=1, keepdims=True))
                - jnp.exp(jnp.sum(lam[2:3] * lam[3:4], axis=1, keepdims=True)) + lam_init)
    acc = acc_scr[...]
    o12 = acc[:, :hd] / acc[:, hd:hd + 1]
    o = o12[:tq] - lam_full * o12[tq:]
    o = o * lax.rsqrt(jnp.mean(o * o, axis=1, keepdims=True) + LN_EPS)
    o = o * (subln_ref[...] * (1.0 - lam_init))
    o_ref[...] = o.astype(BF16)


def _attn_call(q, kt, v, kct, vc, lam, subln_row, lam_init):
    bsz, l, d = q.shape
    lc = vc.shape[1]
    hd = d // ATT_HEADS
    tq = min(ATT_TQ, l)
    tk = min(ATT_TK, l)
    return pl.pallas_call(
        functools.partial(_attn_kernel, tq=tq, tk=tk, lam_init=lam_init),
        out_shape=jax.ShapeDtypeStruct((bsz, l, d), BF16),
        grid=(bsz, ATT_HEADS, l // tq),
        in_specs=[
            pl.BlockSpec((None, tq, hd), lambda b, h, i: (b, i, h)),
            pl.BlockSpec((None, hd, l), lambda b, h, i: (b, h, 0)),
            pl.BlockSpec((None, l, hd), lambda b, h, i: (b, 0, h)),
            pl.BlockSpec((None, hd, lc), lambda b, h, i: (b, h, 0)),
            pl.BlockSpec((None, lc, hd), lambda b, h, i: (b, 0, h)),
            pl.BlockSpec(lam.shape, lambda b, h, i: (0, 0)),
            pl.BlockSpec(subln_row.shape, lambda b, h, i: (0, 0)),
        ],
        out_specs=pl.BlockSpec((None, tq, hd), lambda b, h, i: (b, i, h)),
        scratch_shapes=[pltpu.VMEM((2 * tq, LANES), F32), pltpu.VMEM((2 * tq, hd + LANES), F32),
                        pltpu.VMEM((2, 2 * tq, tk), F32), pltpu.VMEM((2, 2 * tq, tk), BF16),
                        pltpu.VMEM((2, 2 * tq, LANES), F32)],
        compiler_params=_cparams(("arbitrary", "arbitrary", "arbitrary")),
        name="diff_attn",
    )(q, kt, v, kct, vc, lam, subln_row)


def _rope_tables(l, hd):
    dh = hd // 2
    half = dh // 2
    rows = l // GRID_W
    row = jnp.repeat(jnp.arange(rows), GRID_W).astype(F32)
    col = jnp.tile(jnp.arange(GRID_W), rows).astype(F32)
    inv = ROPE_THETA ** (-jnp.arange(0, half, 2, dtype=F32) / half)
    ar = row[:, None] * inv
    ac = col[:, None] * inv
    ang = jnp.concatenate([ar, ar, ac, ac], axis=-1)
    cos, sin = jnp.cos(ang), jnp.sin(ang)
    first = (np.arange(dh) % half) < half // 2
    sin_a = jnp.where(first, -sin, 0.0)
    sin_b = jnp.where(first, 0.0, sin)
    return jnp.stack([jnp.tile(cos, (1, 2)), jnp.tile(sin_a, (1, 2)), jnp.tile(sin_b, (1, 2))])


def _attention_mixer(x, ctx, sc1, sh1, csc1, csh1, wqkv, lam, subln, layer_idx):
    bsz, l, d = x.shape
    hd = d // ATT_HEADS
    lam_init = 0.8 - 0.6 * math.exp(-0.3 * layer_idx)
    w = wqkv.astype(BF16)
    q, kt, v = _qkv_call(x, sc1, sh1, w, _rope_tables(l, hd), ((0, False), (1, True), (2, False)),
                         (hd // 2) ** -0.5 * math.log2(math.e))
    ones = jnp.ones((bsz, 1, 1), F32)
    kct, vc = _qkv_call(ctx, csc1[None, None, :] * ones, csh1[None, None, :] * ones, w, None,
                        ((1, True), (2, False)), 1.0)
    return _attn_call(q, kt, v, kct, vc, lam, subln.reshape(1, hd), lam_init)


def _halo_specs(tm, d, l):
    per = tm // HALO
    last = l // HALO - 1
    return [
        pl.BlockSpec((None, tm, d), lambda b, i: (b, i, 0)),
        pl.BlockSpec((None, HALO, d), lambda b, i: (b, jnp.maximum(i * per - 1, 0), 0)),
        pl.BlockSpec((None, HALO, d), lambda b, i: (b, jnp.minimum((i + 1) * per, last), 0)),
    ]


def _pool_kernel(x_ref, xp_ref, xn_ref, sc_ref, sh_ref, o_ref, ext, *, tm, l):
    i = pl.program_id(1)
    nt = pl.num_programs(1)
    sc = 1.0 + sc_ref[...]
    sh = sh_ref[...]
    ext[HALO:HALO + tm, :] = x_ref[...] * sc + sh
    ext[0:HALO, :] = jnp.where(i > 0, xp_ref[...] * sc + sh, 0.0)
    ext[HALO + tm:, :] = jnp.where(i < nt - 1, xn_ref[...] * sc + sh, 0.0)
    gw = x_ref.shape[1] // len(POOL_WINDOWS)
    t = i * tm + lax.broadcasted_iota(I32, (tm, gw), 0)
    for g, win in enumerate(POOL_WINDOWS):
        c0 = g * gw
        acc = None
        for j in range(-(win // 2), win - win // 2):
            part = ext[HALO + j:HALO + j + tm, c0:c0 + gw]
            acc = part if acc is None else acc + part
        lo = jnp.maximum(t - win // 2, 0)
        hi = jnp.minimum(t - win // 2 + win, l)
        mean = acc / (hi - lo).astype(F32)
        o_ref[:, c0:c0 + gw] = (mean - ext[HALO:HALO + tm, c0:c0 + gw]).astype(BF16)


def _pool_call(x, sc1, sh1):
    bsz, l, d = x.shape
    tm = min(ROW_TILE, l)
    bvec = lambda: pl.BlockSpec((None, 1, d), lambda b, i: (b, 0, 0))
    return pl.pallas_call(
        functools.partial(_pool_kernel, tm=tm, l=l),
        out_shape=jax.ShapeDtypeStruct((bsz, l, d), BF16),
        grid=(bsz, l // tm),
        in_specs=_halo_specs(tm, d, l) + [bvec(), bvec()],
        out_specs=pl.BlockSpec((None, tm, d), lambda b, i: (b, i, 0)),
        scratch_shapes=[pltpu.VMEM((tm + 2 * HALO, d), F32)],
        compiler_params=_cparams(("arbitrary", "arbitrary")),
        name="pool",
    )(x, x, x, sc1, sh1)


KRON = SUBLANES
INNER = 128
N_RB = INNER // KRON


def _trig(num, den):
    ang = jnp.asarray(np.mod(num, den), F32) * (2.0 * math.pi / den)
    return jnp.cos(ang), jnp.sin(ang)


def _kron_rows(m):
    p, q = m.shape
    rep_r = jnp.asarray(np.kron(np.eye(p), np.ones((KRON, 1))), F32)
    rep_c = jnp.asarray(np.kron(np.eye(q), np.ones((1, KRON))), F32)
    hi = lax.Precision.HIGHEST
    big = jnp.dot(jnp.dot(rep_r, m, precision=hi), rep_c, precision=hi)
    r = lax.broadcasted_iota(I32, big.shape, 0) % KRON
    c = lax.broadcasted_iota(I32, big.shape, 1) % KRON
    return jnp.where(r == c, big, 0.0)


def _outer_kernel(*refs, n_in):
    m_ref, x_refs, o_ref = refs[0], refs[1:1 + n_in], refs[1 + n_in]
    acc = None
    col = 0
    for x_ref in x_refs:
        a, j, d = x_ref.shape
        x = x_ref[...].reshape(a * j, d).astype(BF16)
        part = _mm(m_ref[:, col:col + a * j], x)
        acc = part if acc is None else acc + part
        col += a * j
    o_ref[...] = acc.astype(o_ref.dtype)


def _outer_fwd_call(m, xs, a_out):
    bsz, rows, d = xs[0].shape
    a_in = rows // INNER
    xs4 = [x.reshape(bsz, a_in, N_RB, KRON, d) for x in xs]
    return pl.pallas_call(
        functools.partial(_outer_kernel, n_in=len(xs)),
        out_shape=jax.ShapeDtypeStruct((bsz, N_RB, a_out * 2 * KRON, d), BF16),
        grid=(bsz, N_RB),
        in_specs=[pl.BlockSpec(m.shape, lambda b, r: (0, 0))]
        + [pl.BlockSpec((None, a_in, None, KRON, d), lambda b, r: (b, 0, r, 0, 0)) for _ in xs],
        out_specs=pl.BlockSpec((None, None, a_out * 2 * KRON, d), lambda b, r: (b, r, 0, 0)),
        compiler_params=_cparams(("arbitrary", "arbitrary")),
        name="dft_outer",
    )(m, *xs4)


def _fn_chan_kernel(x_ref, sc_ref, sh_ref, cs_ref, pr_ref, pi_ref, *, gw):
    h = (x_ref[...] * (1.0 + sc_ref[...]) + sh_ref[...]).astype(BF16)
    for g in range(h.shape[1] // gw):
        z = _mm(h[:, g * gw:(g + 1) * gw], cs_ref[...])
        pr_ref[:, g * gw:(g + 1) * gw] = z[:, :gw]
        pi_ref[:, g * gw:(g + 1) * gw] = z[:, gw:]


def _fn_chan_call(x, sc1, sh1):
    bsz, l, d = x.shape
    gw = d // FFT_GROUPS
    tm = min(ROW_TILE, l)
    cc, ss = _trig(np.outer(np.arange(gw), np.arange(gw)), gw)
    cs = jnp.concatenate([cc, -ss], axis=1).astype(BF16)
    bvec = lambda: pl.BlockSpec((None, 1, d), lambda b, i: (b, 0, 0))
    tile = lambda: pl.BlockSpec((None, tm, d), lambda b, i: (b, i, 0))
    return pl.pallas_call(
        functools.partial(_fn_chan_kernel, gw=gw),
        out_shape=(jax.ShapeDtypeStruct((bsz, l, d), F32), jax.ShapeDtypeStruct((bsz, l, d), F32)),
        grid=(bsz, l // tm),
        in_specs=[tile(), bvec(), bvec(), pl.BlockSpec(cs.shape, lambda b, i: (0, 0))],
        out_specs=(tile(), tile()),
        compiler_params=_cparams(("arbitrary", "arbitrary")),
        name="fnet_channel_dft",
    )(x, sc1, sh1, cs)


def _fn_inner_kernel(m_ref, x_ref, o_ref):
    rb, j1, cj, d = x_ref.shape
    x = x_ref[...].reshape(rb * j1 * cj, d)
    y = _mm(m_ref[...], x)
    o_ref[...] = y.reshape(o_ref.shape)


def _fn_inner_call(m2, spec, a):
    bsz, _, _, d = spec.shape
    spec5 = spec.reshape(bsz, N_RB, a, 2 * KRON, d)
    out = pl.pallas_call(
        _fn_inner_kernel,
        out_shape=jax.ShapeDtypeStruct((bsz, INNER, a, d), F32),
        grid=(a // KRON, bsz),
        in_specs=[
            pl.BlockSpec((None,) + m2.shape[1:], lambda s, b: (s, 0, 0)),
            pl.BlockSpec((None, N_RB, KRON, 2 * KRON, d), lambda s, b: (b, 0, s, 0, 0)),
        ],
        out_specs=pl.BlockSpec((None, INNER, KRON, d), lambda s, b: (b, 0, s, 0)),
        compiler_params=_cparams(("arbitrary", "arbitrary")),
        name="fnet_inner_dft",
    )(m2, spec5)
    return out.reshape(bsz, INNER * a, d)


def _fnet_tables(l, gw):
    a = l // INNER
    k1 = np.arange(a)
    c1, s1 = _trig(np.outer(k1, k1), a)
    m1 = jnp.stack([jnp.stack([c1, s1], axis=1), jnp.stack([-s1, c1], axis=1)], axis=1)
    m1k = _kron_rows(m1.reshape(2 * a, 2 * a))
    rb, j1c, cc, jj = np.meshgrid(np.arange(N_RB), np.arange(KRON), np.arange(2), np.arange(KRON), indexing="ij")
    l2 = (rb * KRON + jj).reshape(-1)
    j1c, cc = j1c.reshape(-1), cc.reshape(-1)
    norm = 1.0 / math.sqrt(l * gw)
    ca, sa = _trig(np.outer(np.arange(INNER), l2), INNER)
    p_tab = jnp.where(cc == 0, ca, sa) * norm
    q_tab = jnp.where(cc == 0, -sa, ca) * norm
    k1_col = KRON * np.arange(a // KRON)[:, None] + j1c[None, :]
    cb, sb = _trig(k1_col * l2[None, :], l)
    t3 = p_tab[None] * cb[:, None, :] + q_tab[None] * sb[:, None, :]
    mask = jnp.asarray(j1c[None, :] == np.arange(KRON)[:, None], F32)
    m2 = (t3[:, :, None, :] * mask[None, None]).reshape(a // KRON, INNER * KRON, l2.shape[0])
    return m1k.astype(BF16), m2.astype(BF16)


def _fourier_mixer(x, sc1, sh1):
    bsz, l, d = x.shape
    a = l // INNER
    m1k, m2 = _fnet_tables(l, d // FFT_GROUPS)
    pr, pi = _fn_chan_call(x, sc1, sh1)
    spec = _outer_fwd_call(m1k, [pr, pi], a)
    return _fn_inner_call(m2, spec, a)


def _hy_front_kernel(x_ref, xp_ref, xn_ref, sc_ref, sh_ref, w_ref, bin_ref, cw_ref, cb_ref,
                     vg_ref, x0_ref, hext, zscr, *, tm, l, d):
    i = pl.program_id(1)
    sc = 1.0 + sc_ref[...]
    sh = sh_ref[...]
    hext[HALO:HALO + tm, :] = (x_ref[...] * sc + sh).astype(BF16)
    hext[0:HALO, :] = (xp_ref[...] * sc + sh).astype(BF16)
    hext[HALO + tm:, :] = (xn_ref[...] * sc + sh).astype(BF16)
    pos = i * tm - HALO + lax.broadcasted_iota(I32, (tm + 2 * HALO, MXU_DIM), 0)

    def conv(col):
        z = _mm(hext[...], w_ref[:, col:col + MXU_DIM]) + bin_ref[:, col:col + MXU_DIM]
        zscr[...] = jnp.where(pos >= 0, jnp.where(pos < l, z, 0.0), 0.0)
        cw = cw_ref[:, col:col + MXU_DIM]
        return (zscr[HALO - 1:HALO - 1 + tm, :] * cw[0:1] + zscr[HALO:HALO + tm, :] * cw[1:2]
                + zscr[HALO + 1:HALO + 1 + tm, :] * cw[2:3] + cb_ref[:, col:col + MXU_DIM])

    for c in range(d // MXU_DIM):
        cols = slice(c * MXU_DIM, (c + 1) * MXU_DIM)
        x0_ref[:, cols] = conv(c * MXU_DIM)
        x1 = conv(d + c * MXU_DIM)
        v = conv(2 * d + c * MXU_DIM)
        vg_ref[:, cols] = v * x1


def _hy_front_call(x, sc1, sh1, win, bin_, conv_w, conv_b):
    bsz, l, d = x.shape
    tm = min(ROW_TILE, l)
    n3 = win.shape[1]
    bvec = lambda: pl.BlockSpec((None, 1, d), lambda b, i: (b, 0, 0))
    full = lambda a: pl.BlockSpec(a.shape, lambda b, i: (0, 0))
    tile = lambda: pl.BlockSpec((None, tm, d), lambda b, i: (b, i, 0))
    args = (win.astype(BF16), bin_.reshape(1, n3), conv_w, conv_b.reshape(1, n3))
    return pl.pallas_call(
        functools.partial(_hy_front_kernel, tm=tm, l=l, d=d),
        out_shape=(jax.ShapeDtypeStruct((bsz, l, d), F32), jax.ShapeDtypeStruct((bsz, l, d), F32)),
        grid=(bsz, l // tm),
        in_specs=_halo_specs(tm, d, l) + [bvec(), bvec()] + [full(a) for a in args],
        out_specs=(tile(), tile()),
        scratch_shapes=[pltpu.VMEM((tm + 2 * HALO, d), BF16), pltpu.VMEM((tm + 2 * HALO, MXU_DIM), F32)],
        compiler_params=_cparams(("arbitrary", "arbitrary")),
        name="hyena_front",
    )(x, x, x, sc1, sh1, *args)


def _hy_filter_kernel(z_ref, f1_ref, fb1_ref, f2_ref, fb2_ref, f3_ref, fr_ref, del_ref, o_ref, *, tf, l):
    z = z_ref[...]
    fr = fr_ref[...]
    a = jnp.sin(fr * (_mm(z, f1_ref[...]) + fb1_ref[...]))
    a = jnp.sin(fr * (_mm(a, f2_ref[...]) + fb2_ref[...]))
    k = _mm(a, f3_ref[...])
    window = jnp.exp(-z[:, 0:1] * del_ref[...])
    row = pl.program_id(0) * tf + lax.broadcasted_iota(I32, k.shape, 0)
    o_ref[...] = jnp.where(row == l, 0.0, k * window)


def _hy_filter_call(l, d, f1, fb1, f2, fb2, f3, freq):
    n = 2 * l
    tf = min(ROW_TILE, l)
    hid = f1.shape[1]
    pos = jnp.concatenate([jnp.arange(l, dtype=F32), jnp.zeros((1,), F32),
                           jnp.arange(l - 1, 0, -1, dtype=F32)])
    bands = jnp.linspace(1e-4, HY_BANDS - 1, HY_BANDS, dtype=F32)
    ang = (2.0 * math.pi / l) * pos[:, None] * bands[None, :]
    feat = jnp.concatenate([(pos / l)[:, None], jnp.cos(ang), jnp.sin(ang)], axis=-1)
    kin = 64
    feat = jnp.pad(feat, ((0, 0), (0, kin - feat.shape[1])))
    f1p = jnp.pad(f1, ((0, kin - f1.shape[0]), (0, 0)))
    deltas = jnp.abs(jnp.linspace(math.log(HY_TARGET) / HY_SLOW, math.log(HY_TARGET) / HY_FAST, d,
                                  dtype=F32)).reshape(1, d)
    half = l // tf
    full = lambda a: pl.BlockSpec(a.shape, lambda i: (0, 0))
    args = (f1p, fb1.reshape(1, hid), f2, fb2.reshape(1, hid))
    tail = (freq.reshape(1, hid), jnp.asarray(deltas))
    return pl.pallas_call(
        functools.partial(_hy_filter_kernel, tf=tf, l=l),
        out_shape=jax.ShapeDtypeStruct((n, d), F32),
        grid=(n // tf,),
        in_specs=[pl.BlockSpec((tf, kin), lambda i: (i, 0))] + [full(a) for a in args]
        + [pl.BlockSpec((hid, d), lambda i: (0, i // half))] + [full(a) for a in tail],
        out_specs=pl.BlockSpec((tf, d), lambda i: (i, 0)),
        compiler_params=_cparams(("arbitrary",)),
        name="hyena_filter",
    )(feat, *args, f3, *tail)


def _hy_kspec_kernel(mf_ref, x_ref, o_ref):
    rb, cj, d = x_ref.shape
    o_ref[...] = _mm(mf_ref[...], x_ref[...].reshape(rb * cj, d))


def _hy_kspec_call(mf2, spec, a):
    _, _, _, d = spec.shape
    spec5 = spec.reshape(1, N_RB, a, 2 * KRON, d)
    return pl.pallas_call(
        _hy_kspec_kernel,
        out_shape=jax.ShapeDtypeStruct((a, 2 * INNER, d), F32),
        grid=(a,),
        in_specs=[
            pl.BlockSpec((None, 2 * INNER, 2 * INNER), lambda k: (k, 0, 0)),
            pl.BlockSpec((None, N_RB, None, 2 * KRON, d), lambda k: (0, 0, k, 0, 0)),
        ],
        out_specs=pl.BlockSpec((None, 2 * INNER, d), lambda k: (k, 0, 0)),
        compiler_params=_cparams(("arbitrary",)),
        name="hyena_filter_spectrum",
    )(mf2, spec5)


def _hy_mid_kernel(mf_ref, mi_ref, k_ref, x_ref, o_ref):
    bsz, rb, cj, d = x_ref.shape
    kk = k_ref[...]
    kr, ki = kk[:INNER], kk[INNER:]
    for b in range(bsz):
        v = _mm(mf_ref[...], x_ref[b].reshape(rb * cj, d))
        vr, vi = v[:INNER], v[INNER:]
        z = jnp.concatenate([vr * kr - vi * ki, vr * ki + vi * kr], axis=0).astype(BF16)
        y = _mm(mi_ref[...], z)
        o_ref[b] = y.astype(BF16).reshape(rb, cj, d)


def _hy_mid_call(mf2, mi1, kspec, spec, a):
    bsz, _, _, d = spec.shape
    spec5 = spec.reshape(bsz, N_RB, a, 2 * KRON, d)
    mat = lambda: pl.BlockSpec((None, 2 * INNER, 2 * INNER), lambda k: (k, 0, 0))
    blk = lambda: pl.BlockSpec((bsz, N_RB, None, 2 * KRON, d), lambda k: (0, 0, k, 0, 0))
    out = pl.pallas_call(
        _hy_mid_kernel,
        out_shape=jax.ShapeDtypeStruct(spec5.shape, BF16),
        grid=(a,),
        in_specs=[mat(), mat(), pl.BlockSpec((None, 2 * INNER, d), lambda k: (k, 0, 0)), blk()],
        out_specs=blk(),
        compiler_params=_cparams(("arbitrary",)),
        name="hyena_spectral_product",
    )(mf2, mi1, kspec, spec5)
    return out.reshape(spec.shape)


def _hy_out_kernel(m_ref, b_ref, vg_ref, x0_ref, skip_ref, o_ref):
    a2, j, d = vg_ref.shape
    y = _mm(m_ref[...], b_ref[...])
    vg = vg_ref[...].reshape(a2 * j, d)
    x0 = x0_ref[...].reshape(a2 * j, d)
    o_ref[...] = ((y + vg * skip_ref[...]) * x0).reshape(a2, j, d)


def _hy_out_call(mi2, spec, vg, x0, skip):
    bsz, l, d = vg.shape
    a2 = l // INNER
    view = lambda t: t.reshape(bsz, a2, N_RB, KRON, d)
    blk = lambda: pl.BlockSpec((None, a2, None, KRON, d), lambda b, r: (b, 0, r, 0, 0))
    out = pl.pallas_call(
        _hy_out_kernel,
        out_shape=jax.ShapeDtypeStruct((bsz, a2, N_RB, KRON, d), F32),
        grid=(bsz, N_RB),
        in_specs=[
            pl.BlockSpec(mi2.shape, lambda b, r: (0, 0)),
            pl.BlockSpec((None, None) + spec.shape[2:], lambda b, r: (b, r, 0, 0)),
            blk(), blk(),
            pl.BlockSpec((1, d), lambda b, r: (0, 0)),
        ],
        out_specs=blk(),
        compiler_params=_cparams(("arbitrary", "arbitrary")),
        name="hyena_inverse_outer",
    )(mi2, spec, view(vg), view(x0), skip.reshape(1, d))
    return out.reshape(bsz, l, d)


def _hyena_tables(l):
    n = 2 * l
    a = n // INNER
    kb = np.arange(a)
    cf, sf = _trig(np.outer(kb, kb), a)
    fwd = jnp.stack([cf, -sf], axis=1)
    mf1 = _kron_rows(fwd[:, :, :a // 2].reshape(2 * a, a // 2))
    mk1 = _kron_rows(fwd.reshape(2 * a, a))
    inv = jnp.stack([cf, -sf], axis=2)[:a // 2] * (1.0 / n)
    mi2 = _kron_rows(inv.reshape(a // 2, 2 * a))
    rb, cc, jj = np.meshgrid(np.arange(N_RB), np.arange(2), np.arange(KRON), indexing="ij")
    li = (rb * KRON + jj).reshape(-1)
    cc = cc.reshape(-1)
    ca, sa = _trig(np.outer(np.arange(INNER), li), INNER)
    p_tab = jnp.concatenate([jnp.where(cc == 0, ca, sa), jnp.where(cc == 0, -sa, ca)], axis=0)
    q_tab = jnp.concatenate([jnp.where(cc == 0, -sa, ca), jnp.where(cc == 0, -ca, -sa)], axis=0)
    cb, sb = _trig(np.outer(kb, li), n)
    mf2 = (p_tab[None] * cb[:, None, :] + q_tab[None] * sb[:, None, :]).astype(BF16)
    mi1 = (p_tab.T[None] * cb[:, :, None] + q_tab.T[None] * sb[:, :, None]).astype(BF16)
    return (mf1.astype(BF16), mk1.astype(BF16), mi2.astype(BF16), mf2, mi1)


def _hyena_mixer(x, sc1, sh1, win, bin_, conv_w, conv_b, f1, fb1, f2, fb2, f3, freq, skip):
    bsz, l, d = x.shape
    a = 2 * l // INNER
    mf1, mk1, mi2, mf2, mi1 = _hyena_tables(l)
    vg, x0 = _hy_front_call(x, sc1, sh1, win, bin_, conv_w, conv_b)
    kfull = _hy_filter_call(l, d, f1, fb1, f2, fb2, f3, freq)
    kspec = _hy_kspec_call(mf2, _outer_fwd_call(mk1, [kfull[None]], a), a)
    spec = _outer_fwd_call(mf1, [vg], a)
    spec = _hy_mid_call(mf2, mi1, kspec, spec, a)
    return _hy_out_call(mi2, spec, vg, x0, skip)


def kernel(x, c, ctx, c_ctx, mod_w, mod_b, ln_g, ln_b, attn_wqkv, attn_wo, attn_lam, attn_subln, pool_w,
           pool_scale, fnet_w, fnet_b, hy_win, hy_bin, hy_conv_w, hy_conv_b, hy_f1, hy_fb1, hy_f2, hy_fb2,
           hy_f3, hy_freq, hy_skip, hy_wo, hy_bo, router_w, router_b, moe_wg, moe_wu, moe_wd):
    bsz, l, d = x.shape
    depth = mod_w.shape[0]
    n_mixers = 4
    assert depth <= n_mixers
    assert bsz + 1 <= SUBLANES
    alpha = (2 * depth) ** 0.25
    c8 = jnp.zeros((SUBLANES, d), F32).at[:bsz].set(c).at[bsz].set(c_ctx)
    mod = _mod_call(c8, mod_w, mod_b)
    rwt = router_w.T.astype(BF16)
    rb = router_b.reshape(-1, 1)
    zeros = jnp.zeros((1, d), F32)
    ones = jnp.ones((1, d), F32)
    xl = x
    for i in range(depth):
        kind, j = i % n_mixers, i // n_mixers
        ml = mod[i, :bsz]
        sh1, sc1, g1, sh2, sc2, g2 = [ml[:, k * d:(k + 1) * d][:, None, :] for k in range(6)]
        bias, scale = zeros, ones
        if kind == 0:
            mc = mod[i, bsz]
            a = _attention_mixer(xl, ctx, sc1, sh1, mc[d:2 * d], mc[:d], attn_wqkv[j], attn_lam[j],
                                 attn_subln[j], i)
            w = attn_wo[j]
        elif kind == 1:
            a = _pool_call(xl, sc1, sh1)
            w = jax.scipy.linalg.block_diag(*[pool_w[j, g] for g in range(pool_w.shape[1])])
            scale = pool_scale[j].reshape(1, d)
        elif kind == 2:
            a = _fourier_mixer(xl, sc1, sh1)
            w, bias = fnet_w[j], fnet_b[j].reshape(1, d)
        else:
            a = _hyena_mixer(xl, sc1, sh1, hy_win[j], hy_bin[j], hy_conv_w[j], hy_conv_b[j], hy_f1[j],
                             hy_fb1[j], hy_f2[j], hy_fb2[j], hy_f3[j], hy_freq[j], hy_skip[j])
            w, bias = hy_wo[j], hy_bo[j].reshape(1, d)
        x1, tokx, ri, cnt = _post_call(a, w.astype(BF16), bias, scale, xl, g1, ln_g[i, 0:1], ln_b[i, 0:1],
                                       sc2, sh2, rwt, rb, alpha)
        xl = _moe(x1, tokx, ri, cnt, g2, ln_g[i, 1:2], ln_b[i, 1:2], moe_wg[i].astype(BF16),
                  moe_wu[i].astype(BF16), moe_wd[i].astype(BF16), alpha)
    return xl
```
